```python
import jax, jax.numpy as jnp
from jax import lax
import numpy as np

D_MODEL = 1024
BATCH = 8
SEQ = 2048
DEPTH = 2
DEC_BATCH = 32
DEC_SEQ = 4
PAST_LEN = 16384
PAGE_SIZE = 128

CONV_CH = 512
CONV_WIDTH = 31
N_SLOTS = 8
HEAD_DIM = 64
ATT_WIDTH = N_SLOTS * HEAD_DIM
WINDOWS = (128, 512, 2048)
DILATIONS = (1, 4, 16)
N_GROUPS = 3
SPAN = 128
BLOCK = 128
D_FF = -(-8 * D_MODEL // (3 * 256)) * 256
DN_ALPHA = (2 * DEPTH) ** 0.25
DN_BETA = (8 * DEPTH) ** -0.25
LN_EPS = 1e-5
QKV_OFF = 2 * CONV_CH
GATE_OFF = QKV_OFF + 3 * N_GROUPS * ATT_WIDTH
IN_WIDTH = GATE_OFF + 2 * D_MODEL

kernel_name = 'dilated_conformer_hybrid_step'


def layer_norm(x, g, b):
    xf = x.astype(jnp.float32)
    xc = xf - jnp.mean(xf, axis=-1, keepdims=True)
    var = jnp.mean(xc * xc, axis=-1, keepdims=True)
    return (xc * lax.rsqrt(var + LN_EPS) * g + b).astype(x.dtype)


def alibi_slopes():
    return jnp.asarray(2.0 ** (-8.0 * (np.arange(N_SLOTS) + 1) / N_SLOTS), dtype=jnp.float32)


def conv_branch(conv_in, w_dw, b_dw, ln_g, ln_b, w_pc, b_pc):
    y = lax.conv_general_dilated(conv_in, w_dw[:, None, :], window_strides=(1,), padding='VALID',
                                 dimension_numbers=('NWC', 'WIO', 'NWC'),
                                 feature_group_count=CONV_CH) + b_dw
    y = jax.nn.silu(layer_norm(y, ln_g, ln_b))
    return y @ w_pc + b_pc


def dilated_attn_prompt(q, k, v, dil, slopes):
    B, S, H, Dh = q.shape
    n = S // dil
    nb = -(-n // BLOCK)
    n_pad = nb * BLOCK

    def to_sub(t):
        t = t.reshape(B, n, dil, H, Dh).transpose(0, 2, 1, 3, 4).reshape(B * dil, n, H, Dh)
        return jnp.pad(t, ((0, 0), (0, n_pad - n), (0, 0), (0, 0)))

    def band(t):
        tp = jnp.pad(t, ((0, 0), (BLOCK, 0), (0, 0), (0, 0))).reshape(B * dil, nb + 1, BLOCK, H, Dh)
        return jnp.concatenate([tp[:, :-1], tp[:, 1:]], axis=2)

    qb = to_sub(q).reshape(B * dil, nb, BLOCK, H, Dh)
    kb = band(to_sub(k))
    vb = band(to_sub(v))
    s = jnp.einsum('xcqhd,xckhd->xchqk', qb, kb, preferred_element_type=jnp.float32) * (HEAD_DIM ** -0.5)
    rel = BLOCK + jnp.arange(BLOCK)[:, None] - jnp.arange(2 * BLOCK)[None, :]
    key_sub = (jnp.arange(nb)[:, None] - 1) * BLOCK + jnp.arange(2 * BLOCK)[None, :]
    valid = ((rel >= 0) & (rel <= SPAN))[None] & (key_sub >= 0)[:, None, :]
    bias = -slopes[:, None, None] * (rel * dil).astype(jnp.float32)[None]
    s = jnp.where(valid[None, :, None], s + bias, -jnp.inf)
    lse = jax.nn.logsumexp(s, axis=-1)
    p = jnp.exp(s - lse[..., None]).astype(v.dtype)
    o = jnp.einsum('xchqk,xckhd->xcqhd', p, vb, preferred_element_type=jnp.float32)
    o = o.reshape(B, dil, n_pad, H, Dh)[:, :, :n].transpose(0, 2, 1, 3, 4).reshape(B, S, H, Dh)
    lse = lse.transpose(0, 1, 3, 2).reshape(B, dil, n_pad, H)[:, :, :n]
    lse = lse.transpose(0, 2, 1, 3).reshape(B, S, H)
    return o, lse


def dilated_attn_sample(q, kc, vc, dil, slopes):
    T = q.shape[1]
    L = kc.shape[1] - T
    j = jnp.arange(SPAN + 1)
    idx = L + jnp.arange(T)[:, None] - j[None, :] * dil
    valid = idx >= 0
    idx = jnp.maximum(idx, 0)
    kg = kc[:, idx]
    vg = vc[:, idx]
    s = jnp.einsum('bthd,btjhd->bthj', q, kg, preferred_element_type=jnp.float32) * (HEAD_DIM ** -0.5)
    bias = -slopes[:, None] * (j * dil).astype(jnp.float32)[None, :]
    s = jnp.where(valid[None, :, None, :], s + bias, -jnp.inf)
    lse = jax.nn.logsumexp(s, axis=-1)
    p = jnp.exp(s - lse[..., None]).astype(vc.dtype)
    o = jnp.einsum('bthj,btjhd->bthd', p, vg, preferred_element_type=jnp.float32)
    return o, lse


def token_mixers(u, mw, slopes, conv_prev, kv_prev):
    (w_in, b_in, w_dw, b_dw, conv_ln_g, conv_ln_b, w_pc, b_pc, w_pa, b_pa, w_out, b_out) = mw
    B, T, _ = u.shape
    a = u @ w_in + b_in
    glu = a[..., :CONV_CH] * jax.nn.sigmoid(a[..., CONV_CH:QKV_OFF])
    qkv = a[..., QKV_OFF:GATE_OFF].reshape(B, T, 3, N_GROUPS, N_SLOTS, HEAD_DIM)
    gate_c = jax.nn.sigmoid(a[..., GATE_OFF:GATE_OFF + D_MODEL])
    gate_a = jax.nn.sigmoid(a[..., GATE_OFF + D_MODEL:])

    if conv_prev is None:
        conv_in = jnp.pad(glu, ((0, 0), (CONV_WIDTH - 1, 0), (0, 0)))
    else:
        conv_in = jnp.concatenate([conv_prev.astype(glu.dtype), glu], axis=1)
    y_conv = conv_branch(conv_in, w_dw, b_dw, conv_ln_g, conv_ln_b, w_pc, b_pc)
    new_conv = conv_in[:, -(CONV_WIDTH - 1):]

    outs, lses, new_kv = [], [], []
    for g in range(N_GROUPS):
        q, k, v = qkv[:, :, 0, g], qkv[:, :, 1, g], qkv[:, :, 2, g]
        kv_new = jnp.stack([k, v], axis=2)
        if kv_prev is None:
            o, lse = dilated_attn_prompt(q, k, v, DILATIONS[g], slopes)
            new_kv.append(kv_new[:, -min(WINDOWS[g], T):])
        else:
            cat = jnp.concatenate([kv_prev[g].astype(kv_new.dtype), kv_new], axis=1)
            o, lse = dilated_attn_sample(q, cat[:, :, 0], cat[:, :, 1], DILATIONS[g], slopes)
            new_kv.append(cat[:, T:])
        outs.append(o)
        lses.append(lse)
    w = jax.nn.softmax(jnp.stack(lses), axis=0)
    o = jnp.sum(w[..., None] * jnp.stack(outs), axis=0).astype(u.dtype)
    y_att = o.reshape(B, T, ATT_WIDTH) @ w_pa + b_pa

    y = (gate_c * y_conv + gate_a * y_att) @ w_out + b_out
    return y, new_kv, new_conv


def block(x, c, lw, slopes, conv_prev, kv_prev):
    (w_ada, b_ada, w_in, b_in, w_dw, b_dw, conv_ln_g, conv_ln_b, w_pc, b_pc, w_pa, b_pa,
     w_out, b_out, ln1_g, ln1_b, w_gate, w_up, w_down, ln2_g, ln2_b) = lw
    mod = (jax.nn.silu(c) @ w_ada + b_ada)[:, None, :]
    sh1, sc1, g1, sh2, sc2, g2 = jnp.split(mod, 6, axis=-1)
    u = x * (1 + sc1) + sh1
    y, new_kv, new_conv = token_mixers(
        u, (w_in, b_in, w_dw, b_dw, conv_ln_g, conv_ln_b, w_pc, b_pc, w_pa, b_pa, w_out, b_out),
        slopes, conv_prev, kv_prev)
    x = layer_norm(DN_ALPHA * x + (1 + g1) * y, ln1_g, ln1_b)
    u = x * (1 + sc2) + sh2
    h = (jax.nn.silu(u @ w_gate) * (u @ w_up)) @ w_down
    x = layer_norm(DN_ALPHA * x + (1 + g2) * h, ln2_g, ln2_b)
    return x, new_kv, new_conv


def setup_inputs(seed: int = 0) -> dict:
    key = jax.random.key(seed)
    keys = list(jax.random.split(key, 32))

    def nrm(shape, scale):
        return jax.random.normal(keys.pop(), shape, jnp.float32) * scale

    L, D = DEPTH, D_MODEL
    buf = [min(w, PAST_LEN) for w in WINDOWS]
    return {
        'x_prompt': nrm((BATCH, SEQ, D), 1.0),
        'x_sample': nrm((DEC_BATCH, DEC_SEQ, D), 1.0),
        'c_prompt': nrm((BATCH, D), 1.0),
        'c_sample': nrm((DEC_BATCH, D), 1.0),
        'cache_kv_g0': nrm((L, DEC_BATCH, buf[0], 2, N_SLOTS, HEAD_DIM), 1.0),
        'cache_kv_g1': nrm((L, DEC_BATCH, buf[1], 2, N_SLOTS, HEAD_DIM), 1.0),
        'cache_kv_g2': nrm((L, DEC_BATCH, buf[2], 2, N_SLOTS, HEAD_DIM), 1.0),
        'state_conv': nrm((L, DEC_BATCH, CONV_WIDTH - 1, CONV_CH), 0.5),
        'w_ada': nrm((L, D, 6 * D), 0.1 * D ** -0.5),
        'b_ada': nrm((L, 6 * D), 0.01),
        'w_in': nrm((L, D, IN_WIDTH), D ** -0.5),
        'b_in': nrm((L, IN_WIDTH), 0.01),
        'w_dw': nrm((L, CONV_WIDTH, CONV_CH), CONV_WIDTH ** -0.5),
        'b_dw': nrm((L, CONV_CH), 0.01),
        'conv_ln_g': 1.0 + nrm((L, CONV_CH), 0.01),
        'conv_ln_b': nrm((L, CONV_CH), 0.01),
        'w_pc': nrm((L, CONV_CH, D), CONV_CH ** -0.5),
        'b_pc': nrm((L, D), 0.01),
        'w_pa': nrm((L, ATT_WIDTH, D), ATT_WIDTH ** -0.5),
        'b_pa': nrm((L, D), 0.01),
        'w_out': nrm((L, D, D), DN_BETA * D ** -0.5),
        'b_out': nrm((L, D), 0.01),
        'ln1_g': 1.0 + nrm((L, D), 0.01),
        'ln1_b': nrm((L, D), 0.01),
        'w_gate': nrm((L, D, D_FF), D ** -0.5),
        'w_up': nrm((L, D, D_FF), D ** -0.5),
        'w_down': nrm((L, D_FF, D), DN_BETA * D_FF ** -0.5),
        'ln2_g': 1.0 + nrm((L, D), 0.01),
        'ln2_b': nrm((L, D), 0.01),
    }


def reference(x_prompt, x_sample, c_prompt, c_sample, cache_kv_g0, cache_kv_g1, cache_kv_g2,
              state_conv, w_ada, b_ada, w_in, b_in, w_dw, b_dw, conv_ln_g, conv_ln_b, w_pc, b_pc,
              w_pa, b_pa, w_out, b_out, ln1_g, ln1_b, w_gate, w_up, w_down, ln2_g, ln2_b):
    slopes = alibi_slopes()
    caches = (cache_kv_g0, cache_kv_g1, cache_kv_g2)
    xp, xs = x_prompt, x_sample
    kv_p = [[], [], []]
    kv_s = [[], [], []]
    conv_p, conv_s = [], []
    for l in range(DEPTH):
        lw = (w_ada[l], b_ada[l], w_in[l], b_in[l], w_dw[l], b_dw[l], conv_ln_g[l], conv_ln_b[l],
              w_pc[l], b_pc[l], w_pa[l], b_pa[l], w_out[l], b_out[l], ln1_g[l], ln1_b[l],
              w_gate[l], w_up[l], w_down[l], ln2_g[l], ln2_b[l])
        xp, nkv_p, nconv_p = block(xp, c_prompt, lw, slopes, None, None)
        xs, nkv_s, nconv_s = block(xs, c_sample, lw, slopes, state_conv[l],
                                   (caches[0][l], caches[1][l], caches[2][l]))
        for g in range(N_GROUPS):
            kv_p[g].append(nkv_p[g])
            kv_s[g].append(nkv_s[g])
        conv_p.append(nconv_p)
        conv_s.append(nconv_s)
    return (xp, xs,
            jnp.stack(kv_p[0]), jnp.stack(kv_p[1]), jnp.stack(kv_p[2]), jnp.stack(conv_p),
            jnp.stack(kv_s[0]), jnp.stack(kv_s[1]), jnp.stack(kv_s[2]), jnp.stack(conv_s))
```

```python
import functools

import numpy as np
import jax
import jax.numpy as jnp
from jax import lax
from jax.experimental import pallas as pl
from jax.experimental.pallas import tpu as pltpu

D_MODEL = 1024
CONV_CH = 512
CONV_WIDTH = 31
N_SLOTS = 8
HEAD_DIM = 64
ATT_WIDTH = N_SLOTS * HEAD_DIM
WINDOWS = (128, 512, 2048)
DILATIONS = (1, 4, 16)
N_GROUPS = 3
SPAN = 128
BLOCK = 128
LN_EPS = 1e-5
QKV_OFF = 2 * CONV_CH
GATE_OFF = QKV_OFF + 3 * N_GROUPS * ATT_WIDTH
KV_WIDTH = 2 * ATT_WIDTH
MOD_WIDTH = 6 * D_MODEL
LANES = 128
NEG = -1e30
Q_ROWS = 8
CONV_HALO = 32
VMEM_LIMIT = 56 * 1024 * 1024

F32 = jnp.float32
BF16 = jnp.bfloat16
GATE_DTYPE = jnp.float32
ATT_OUT_DTYPE = jnp.float32

_NT = (((1,), (1,)), ((), ()))


def _dot(a, b):
    return jnp.dot(a, b, preferred_element_type=F32)


def _dot_nt(a, b):
    return lax.dot_general(a, b, _NT, preferred_element_type=F32)


def _sigmoid(x):
    return 1.0 / (1.0 + jnp.exp(-x))


def _silu(x):
    return x * _sigmoid(x)


def _layer_norm(x, g, b):
    mu = jnp.mean(x, axis=-1, keepdims=True)
    xc = x - mu
    var = jnp.mean(xc * xc, axis=-1, keepdims=True)
    return xc * lax.rsqrt(var + LN_EPS) * g + b


def _mod_slice(m, i):
    return m[:, i * D_MODEL:(i + 1) * D_MODEL]


def _const_spec(shape):
    nd = len(shape)
    return pl.BlockSpec(shape, lambda *_: (0,) * nd, pipeline_mode=pl.Buffered(1))


def _params(n_axes):
    return pltpu.CompilerParams(dimension_semantics=("arbitrary",) * n_axes,
                                vmem_limit_bytes=VMEM_LIMIT)


def _adaln_kernel(c_ref, w_ref, b_ref, o_ref):
    c = c_ref[...]
    s = _silu(c).astype(BF16)
    o_ref[0] = _dot(s, w_ref[0].astype(BF16)) + b_ref[0]


def _adaln(c_all, w_ada, b_ada):
    depth, d, width = w_ada.shape
    rows = c_all.shape[0]
    tn = 1024
    return pl.pallas_call(
        _adaln_kernel,
        grid=(depth, width // tn),
        in_specs=[pl.BlockSpec((rows, d), lambda l, j: (0, 0)),
                  pl.BlockSpec((1, d, tn), lambda l, j: (l, 0, j)),
                  pl.BlockSpec((1, 1, tn), lambda l, j: (l, 0, j))],
        out_specs=pl.BlockSpec((1, rows, tn), lambda l, j: (l, 0, j)),
        out_shape=jax.ShapeDtypeStruct((depth, rows, width), F32),
        compiler_params=_params(2),
        name="adaln",
    )(c_all, w_ada, b_ada.reshape(depth, 1, width))


def _modulated(x_ref, mod_ref, shift_i, scale_i):
    m = mod_ref[0]
    x = x_ref[0]
    return (x * (1.0 + _mod_slice(m, scale_i)) + _mod_slice(m, shift_i)).astype(BF16)


def _glu_gates_q(u, w_ref, b_ref):
    def proj(c0, c1):
        return _dot(u, w_ref[:, c0:c1]) + b_ref[:, c0:c1]
    glu = proj(0, CONV_CH) * _sigmoid(proj(CONV_CH, 2 * CONV_CH))
    c0 = 2 * CONV_CH
    gate_c = _sigmoid(proj(c0, c0 + D_MODEL))
    gate_a = _sigmoid(proj(c0 + D_MODEL, c0 + 2 * D_MODEL))
    c0 += 2 * D_MODEL
    qs = [proj(c0 + g * ATT_WIDTH, c0 + (g + 1) * ATT_WIDTH) * (HEAD_DIM ** -0.5)
          for g in range(N_GROUPS)]
    return glu, gate_c, gate_a, qs


def _inproj_a_kernel(x_ref, mod_ref, w_ref, b_ref, glu_ref, gc_ref, ga_ref, *q_refs):
    u = _modulated(x_ref, mod_ref, 0, 1)
    glu, gate_c, gate_a, qs = _glu_gates_q(u, w_ref, b_ref)
    glu_ref[0] = glu
    gc_ref[0] = gate_c.astype(gc_ref.dtype)
    ga_ref[0] = gate_a.astype(ga_ref.dtype)
    tm = u.shape[0]
    lane = lax.broadcasted_iota(jnp.int32, (tm, LANES), 1)
    for q, q_ref in zip(qs, q_refs):
        for p in range(ATT_WIDTH // LANES):
            t = q[:, p * LANES:(p + 1) * LANES]
            q_ref[0, :, (2 * p) * LANES:(2 * p + 1) * LANES] = jnp.where(lane < HEAD_DIM, t, 0.0).astype(BF16)
            q_ref[0, :, (2 * p + 1) * LANES:(2 * p + 2) * LANES] = jnp.where(lane >= HEAD_DIM, t, 0.0).astype(BF16)


def _inproj_a(x, mod, w_a, b_a, tm):
    bsz, s, d = x.shape
    nt = s // tm
    tile = lambda width: pl.BlockSpec((1, tm, width), lambda b, i: (b, i, 0))
    out_shape = ([jax.ShapeDtypeStruct((bsz, s, CONV_CH), F32),
                  jax.ShapeDtypeStruct((bsz, s, d), GATE_DTYPE),
                  jax.ShapeDtypeStruct((bsz, s, d), GATE_DTYPE)]
                 + [jax.ShapeDtypeStruct((bsz, s, 2 * ATT_WIDTH), BF16)] * N_GROUPS)
    return pl.pallas_call(
        _inproj_a_kernel,
        grid=(bsz, nt),
        in_specs=[tile(d),
                  pl.BlockSpec((1, 1, MOD_WIDTH), lambda b, i: (b, 0, 0)),
                  _const_spec(w_a.shape), _const_spec(b_a.shape)],
        out_specs=[tile(CONV_CH), tile(d), tile(d)] + [tile(2 * ATT_WIDTH)] * N_GROUPS,
        out_shape=out_shape,
        compiler_params=_params(2),
        name="inproj_a",
    )(x, mod, w_a, b_a)


def _native_kv(wt_ref, bt_ref, g, u):
    rows = slice(g * KV_WIDTH, (g + 1) * KV_WIDTH)
    return _dot_nt(wt_ref[rows, :], u) + bt_ref[rows, :]


def _inproj_b_kernel(x_ref, mod_ref, w_ref, b_ref, wt_ref, bt_ref,
                     kv0_ref, kv1_ref, kv2_ref, n0_ref, n1_ref, n2_ref, *, tm, seq):
    u = _modulated(x_ref, mod_ref, 0, 1)
    for g, kv_ref in enumerate((kv0_ref, kv1_ref, kv2_ref)):
        cols = slice(g * KV_WIDTH, (g + 1) * KV_WIDTH)
        kv_ref[0] = (_dot(u, w_ref[:, cols]) + b_ref[:, cols]).astype(BF16)
    ti = pl.program_id(1)
    for g, n_ref in enumerate((n0_ref, n1_ref, n2_ref)):
        win = min(WINDOWS[g], seq)
        if win >= tm:
            first = (seq - win) // tm
            if first == 0:
                n_ref[0] = _native_kv(wt_ref, bt_ref, g, u)
            else:
                @pl.when(ti >= first)
                def _(g=g, n_ref=n_ref):
                    n_ref[0] = _native_kv(wt_ref, bt_ref, g, u)
        else:
            @pl.when(ti == seq // tm - 1)
            def _(g=g, n_ref=n_ref, win=win):
                n_ref[0] = _native_kv(wt_ref, bt_ref, g, u[tm - win:, :])


def _inproj_b(x, mod, w_b, b_b, w_bt, b_bt, tm):
    bsz, s, d = x.shape
    nt = s // tm
    tile = lambda width: pl.BlockSpec((1, tm, width), lambda b, i: (b, i, 0))
    nat_specs, nat_shapes = [], []
    for g in range(N_GROUPS):
        win = min(WINDOWS[g], s)
        assert win % tm == 0 or tm % win == 0
        nat_shapes.append(jax.ShapeDtypeStruct((bsz, KV_WIDTH, win), F32))
        if win >= tm:
            first = (s - win) // tm
            nat_specs.append(pl.BlockSpec(
                (1, KV_WIDTH, tm), lambda b, i, first=first: (b, 0, jnp.maximum(i - first, 0))))
        else:
            nat_specs.append(pl.BlockSpec((1, KV_WIDTH, win), lambda b, i: (b, 0, 0)))
    return pl.pallas_call(
        functools.partial(_inproj_b_kernel, tm=tm, seq=s),
        grid=(bsz, nt),
        in_specs=[tile(d),
                  pl.BlockSpec((1, 1, MOD_WIDTH), lambda b, i: (b, 0, 0)),
                  _const_spec(w_b.shape), _const_spec(b_b.shape),
                  _const_spec(w_bt.shape), _const_spec(b_bt.shape)],
        out_specs=[tile(KV_WIDTH)] * N_GROUPS + nat_specs,
        out_shape=[jax.ShapeDtypeStruct((bsz, s, KV_WIDTH), BF16)] * N_GROUPS + nat_shapes,
        compiler_params=_params(2),
        name="inproj_b",
    )(x, mod, w_b, b_b, w_bt, b_bt)


def _inproj_s_kernel(x_ref, mod_ref, xq_ref, modq_ref, w_ref, b_ref, wt_ref, bt_ref,
                     glu_ref, gc_ref, ga_ref, q_ref, kvt_ref):
    u = _modulated(x_ref, mod_ref, 0, 1)
    glu, gate_c, gate_a, qs = _glu_gates_q(u, w_ref, b_ref)
    glu_ref[0] = glu
    gc_ref[0] = gate_c.astype(gc_ref.dtype)
    ga_ref[0] = gate_a.astype(ga_ref.dtype)
    for g, q in enumerate(qs):
        q_ref[0, :, g * ATT_WIDTH:(g + 1) * ATT_WIDTH] = q
    uq = _modulated(xq_ref, modq_ref, 0, 1)
    for g in range(N_GROUPS):
        kvt_ref[g * KV_WIDTH:(g + 1) * KV_WIDTH, :] = _native_kv(wt_ref, bt_ref, g, uq)


def _inproj_s(x, mod, xq, modq, w_a, b_a, w_bt, b_bt):
    _, m, d = x.shape
    full = lambda *shape: _const_spec(shape)
    whole = lambda *shape: pl.BlockSpec(shape, lambda i: (0,) * len(shape))
    return pl.pallas_call(
        _inproj_s_kernel,
        in_specs=[full(1, m, d), full(1, m, MOD_WIDTH), full(1, m, d), full(1, m, MOD_WIDTH),
                  full(*w_a.shape), full(*b_a.shape), full(*w_bt.shape), full(*b_bt.shape)],
        out_specs=[whole(1, m, CONV_CH), whole(1, m, d), whole(1, m, d),
                   whole(1, m, N_GROUPS * ATT_WIDTH), whole(N_GROUPS * KV_WIDTH, m)],
        out_shape=[jax.ShapeDtypeStruct((1, m, CONV_CH), F32),
                   jax.ShapeDtypeStruct((1, m, d), GATE_DTYPE),
                   jax.ShapeDtypeStruct((1, m, d), GATE_DTYPE),
                   jax.ShapeDtypeStruct((1, m, N_GROUPS * ATT_WIDTH), F32),
                   jax.ShapeDtypeStruct((N_GROUPS * KV_WIDTH, m), F32)],
        grid=(1,),
        compiler_params=_params(1),
        name="inproj_s",
    )(x, mod, xq, modq, w_a, b_a, w_bt, b_bt)


def _alibi_slopes():
    return (2.0 ** (-8.0 * (np.arange(N_SLOTS) + 1) / N_SLOTS)).astype(np.float32)


def _prompt_bias(dil):
    qi = np.arange(BLOCK)[:, None]
    kj = np.arange(2 * BLOCK)[None, :]
    rel = BLOCK + qi - kj
    valid = (rel >= 0) & (rel <= SPAN)
    bias = -_alibi_slopes()[:, None, None] * (rel * dil).astype(np.float32)[None]
    return np.where(valid[None], bias, np.float32(NEG)).astype(np.float32)


def _attn_kernel(q_ref, kv_ref, bias_ref, o_ref, lse_ref, *, nb):
    lane = lax.broadcasted_iota(jnp.int32, (BLOCK, LANES), 1)

    def block(c, first):
        if first:
            qrows = pl.ds(0, BLOCK)
            krows = pl.ds(0, BLOCK)
        else:
            r0 = pl.multiple_of(c * BLOCK, BLOCK)
            qrows = pl.ds(r0, BLOCK)
            krows = pl.ds(pl.multiple_of(r0 - BLOCK, BLOCK), 2 * BLOCK)
        for p in range(ATT_WIDTH // LANES):
            kp = kv_ref[0, krows, p * LANES:(p + 1) * LANES]
            vp = kv_ref[0, krows, ATT_WIDTH + p * LANES:ATT_WIDTH + (p + 1) * LANES]
            outs, lses = [], []
            for e in range(2):
                h = 2 * p + e
                qh = q_ref[0, qrows, h * LANES:(h + 1) * LANES]
                s = _dot_nt(qh, kp)
                s = s + (bias_ref[h, :, BLOCK:] if first else bias_ref[h])
                m = jnp.max(s, axis=-1, keepdims=True)
                pexp = jnp.exp(s - m)
                l = jnp.sum(pexp, axis=-1, keepdims=True)
                pv = _dot(pexp.astype(BF16), vp)
                outs.append(pv / l)
                lses.append(m + jnp.log(l))
            cols = slice(p * LANES, (p + 1) * LANES)
            o_ref[0, qrows, cols] = jnp.where(lane < HEAD_DIM, outs[0], outs[1]).astype(o_ref.dtype)
            lse_ref[0, qrows, cols] = jnp.where(lane < HEAD_DIM, lses[0], lses[1])

    block(0, True)
    if nb > 1:
        def body(c, carry):
            block(c, False)
            return carry
        lax.fori_loop(1, nb, body, 0)


def _prompt_attention(qx, kvb, dil):
    bsz, s, _ = qx.shape
    n = s // dil
    assert n % BLOCK == 0
    nb = n // BLOCK
    bias = jnp.asarray(_prompt_bias(dil))
    sub = lambda width: pl.BlockSpec((1, n, width), lambda b, r: (b, 0, r))
    o, lse = pl.pallas_call(
        functools.partial(_attn_kernel, nb=nb),
        grid=(bsz, dil),
        in_specs=[sub(2 * ATT_WIDTH), sub(KV_WIDTH), _const_spec(bias.shape)],
        out_specs=[sub(ATT_WIDTH), sub(ATT_WIDTH)],
        out_shape=[jax.ShapeDtypeStruct((bsz, n, dil * ATT_WIDTH), ATT_OUT_DTYPE),
                   jax.ShapeDtypeStruct((bsz, n, dil * ATT_WIDTH), F32)],
        compiler_params=_params(2),
        name=f"attn_d{dil}",
    )(qx.reshape(bsz, n, dil * 2 * ATT_WIDTH), kvb.reshape(bsz, n, dil * KV_WIDTH), bias)
    return o.reshape(bsz, s, ATT_WIDTH), lse.reshape(bsz, s, ATT_WIDTH)


def _combine_groups(os_, lses):
    m = functools.reduce(jnp.maximum, lses)
    ws = [jnp.exp(l - m) for l in lses]
    den = functools.reduce(lambda a, b: a + b, ws)
    num = functools.reduce(lambda a, b: a + b, [w * o for w, o in zip(ws, os_)])
    return num / den


def _merge_to_ln1(x, m, ybuf, o_att, gate_c, gate_a,
                  wpc_ref, bpc_ref, wpa_ref, bpa_ref, wout_ref, bout_ref, g1_ref, b1_ref, alpha):
    y_conv = _dot(ybuf, wpc_ref[...]) + bpc_ref[...]
    y_att = _dot(o_att.astype(BF16), wpa_ref[...]) + bpa_ref[...]
    y = (gate_c * y_conv + gate_a * y_att).astype(BF16)
    y = _dot(y, wout_ref[...]) + bout_ref[...]
    return _layer_norm(alpha * x + (1.0 + _mod_slice(m, 2)) * y, g1_ref[...], b1_ref[...])


def _ffn_to_ln2(x1, m, wg_ref, wu_ref, wd_ref, g2_ref, b2_ref, alpha, chunk):
    u2 = (x1 * (1.0 + _mod_slice(m, 4)) + _mod_slice(m, 3)).astype(BF16)
    d_ff = wg_ref.shape[1]
    h = None
    for c0 in range(0, d_ff, chunk):
        c1 = min(c0 + chunk, d_ff)
        a = (_silu(_dot(u2, wg_ref[:, c0:c1])) * _dot(u2, wu_ref[:, c0:c1])).astype(BF16)
        part = _dot(a, wd_ref[c0:c1, :])
        h = part if h is None else h + part
    return _layer_norm(alpha * x1 + (1.0 + _mod_slice(m, 5)) * h, g2_ref[...], b2_ref[...])


def _post_a_kernel(x_ref, mod_ref, glu_ref, halo_ref, o0_ref, o1_ref, o2_ref, l0_ref, l1_ref, l2_ref,
                   gc_ref, ga_ref, wdw_ref, bdw_ref, cg_ref, cb_ref,
                   wpc_ref, bpc_ref, wpa_ref, bpa_ref, wout_ref, bout_ref, g1_ref, b1_ref,
                   out_ref, win_ref, ybuf_ref, *, tm, alpha, rb):
    ti = pl.program_id(1)
    halo = halo_ref[0]
    win_ref[0:CONV_HALO, :] = jnp.where(ti > 0, halo, jnp.zeros_like(halo))
    win_ref[CONV_HALO:CONV_HALO + tm, :] = glu_ref[0]
    off = CONV_HALO - (CONV_WIDTH - 1)
    for r0 in range(0, tm, rb):
        acc = jnp.zeros((rb, CONV_CH), F32) + bdw_ref[...]
        for j in range(CONV_WIDTH):
            acc = acc + win_ref[r0 + off + j:r0 + off + j + rb, :] * wdw_ref[j:j + 1, :]
        y = _silu(_layer_norm(acc, cg_ref[...], cb_ref[...]))
        ybuf_ref[r0:r0 + rb, :] = y.astype(BF16)
    o_att = _combine_groups([o0_ref[0].astype(F32), o1_ref[0].astype(F32), o2_ref[0].astype(F32)],
                            [l0_ref[0], l1_ref[0], l2_ref[0]])
    out_ref[0] = _merge_to_ln1(x_ref[0], mod_ref[0], ybuf_ref[...], o_att,
                               gc_ref[0].astype(F32), ga_ref[0].astype(F32),
                               wpc_ref, bpc_ref, wpa_ref, bpa_ref, wout_ref, bout_ref,
                               g1_ref, b1_ref, alpha)


def _post_a(x, mod, glu, os_, lses, gate_c, gate_a, wts, tm, alpha):
    bsz, s, d = x.shape
    nt = s // tm
    assert tm % CONV_HALO == 0
    tile = lambda width: pl.BlockSpec((1, tm, width), lambda b, i: (b, i, 0))
    halo_spec = pl.BlockSpec(
        (1, CONV_HALO, CONV_CH), lambda b, i: (b, jnp.maximum(i * (tm // CONV_HALO) - 1, 0), 0))
    return pl.pallas_call(
        functools.partial(_post_a_kernel, tm=tm, alpha=alpha, rb=64),
        grid=(bsz, nt),
        in_specs=([tile(d), pl.BlockSpec((1, 1, MOD_WIDTH), lambda b, i: (b, 0, 0)),
                   tile(CONV_CH), halo_spec]
                  + [tile(ATT_WIDTH)] * (2 * N_GROUPS) + [tile(d), tile(d)]
                  + [_const_spec(w.shape) for w in wts]),
        out_specs=tile(d),
        out_shape=jax.ShapeDtypeStruct((bsz, s, d), F32),
        scratch_shapes=[pltpu.VMEM((CONV_HALO + tm, CONV_CH), F32),
                        pltpu.VMEM((tm, CONV_CH), BF16)],
        compiler_params=_params(2),
        name="post_a",
    )(x, mod, glu, glu, *os_, *lses, gate_c, gate_a, *wts)


def _post_b_kernel(x_ref, mod_ref, wg_ref, wu_ref, wd_ref, g2_ref, b2_ref, out_ref, *, alpha, chunk):
    out_ref[0] = _ffn_to_ln2(x_ref[0], mod_ref[0], wg_ref, wu_ref, wd_ref, g2_ref, b2_ref, alpha, chunk)


def _post_b(x1, mod, wts, tm, alpha):
    bsz, s, d = x1.shape
    tile = pl.BlockSpec((1, tm, d), lambda b, i: (b, i, 0))
    return pl.pallas_call(
        functools.partial(_post_b_kernel, alpha=alpha, chunk=1024),
        grid=(bsz, s // tm),
        in_specs=[tile, pl.BlockSpec((1, 1, MOD_WIDTH), lambda b, i: (b, 0, 0))]
                 + [_const_spec(w.shape) for w in wts],
        out_specs=tile,
        out_shape=jax.ShapeDtypeStruct((bsz, s, d), F32),
        compiler_params=_params(2),
        name="post_b",
    )(x1, mod, *wts)


def _post_s_kernel(x_ref, mod_ref, state_ref, glu_ref, o_ref, gc_ref, ga_ref,
                   wdw_ref, bdw_ref, cg_ref, cb_ref,
                   wpc_ref, bpc_ref, wpa_ref, bpa_ref, wout_ref, bout_ref, g1_ref, b1_ref,
                   wg_ref, wu_ref, wd_ref, g2_ref, b2_ref,
                   out_ref, nconv_ref, ypre_ref, *, nseq, t_new, alpha, chunk):
    ctx = CONV_WIDTH - 1

    def slab(i):
        if i < ctx:
            return state_ref[0, i]
        return glu_ref[0, (i - ctx) * nseq:(i - ctx + 1) * nseq, :]

    for t in range(t_new):
        acc = jnp.zeros((nseq, CONV_CH), F32) + bdw_ref[...]
        for j in range(CONV_WIDTH):
            acc = acc + slab(t + j) * wdw_ref[j:j + 1, :]
        ypre_ref[t * nseq:(t + 1) * nseq, :] = acc
    for i in range(ctx):
        nconv_ref[0, i] = slab(i + t_new)
    ybuf = _silu(_layer_norm(ypre_ref[...], cg_ref[...], cb_ref[...])).astype(BF16)
    m = mod_ref[0]
    x1 = _merge_to_ln1(x_ref[0], m, ybuf, o_ref[0], gc_ref[0].astype(F32), ga_ref[0].astype(F32),
                       wpc_ref, bpc_ref, wpa_ref, bpa_ref, wout_ref, bout_ref, g1_ref, b1_ref, alpha)
    out_ref[0] = _ffn_to_ln2(x1, m, wg_ref, wu_ref, wd_ref, g2_ref, b2_ref, alpha, chunk)


def _post_s(x, mod, state_l, layer, glu, o_att, gate_c, gate_a, wts, nseq, t_new, alpha):
    _, m, d = x.shape
    ctx = CONV_WIDTH - 1
    full = lambda *shape: _const_spec(shape)
    whole = lambda *shape: pl.BlockSpec(shape, lambda i: (0,) * len(shape))
    state_spec = pl.BlockSpec((1, ctx, nseq, CONV_CH), lambda i: (layer, 0, 0, 0))
    return pl.pallas_call(
        functools.partial(_post_s_kernel, nseq=nseq, t_new=t_new, alpha=alpha, chunk=1024),
        in_specs=[full(1, m, d), full(1, m, MOD_WIDTH), state_spec, full(1, m, CONV_CH),
                  full(1, m, ATT_WIDTH), full(1, m, d), full(1, m, d)]
                 + [full(*w.shape) for w in wts],
        out_specs=[whole(1, m, d), whole(1, ctx, nseq, CONV_CH)],
        out_shape=[jax.ShapeDtypeStruct((1, m, d), F32),
                   jax.ShapeDtypeStruct((1, ctx, nseq, CONV_CH), F32)],
        scratch_shapes=[pltpu.VMEM((m, CONV_CH), F32)],
        grid=(1,),
        compiler_params=_params(1),
        name="post_s",
    )(x, mod, state_l, glu, o_att, gate_c, gate_a, *wts)


def _sample_bias(g, length, t_new):
    dil = DILATIONS[g]
    slopes = _alibi_slopes()[:, None, None]
    t = np.arange(Q_ROWS)[:, None]
    p = np.arange(length)[None, :]
    dist = length + t - p
    valid = (dist % dil == 0) & (dist // dil >= 1) & (dist // dil <= SPAN) & (t < t_new)
    cache = np.where(valid[None], -slopes * dist.astype(np.float32)[None], np.float32(NEG))
    tp = np.arange(LANES)[None, :] - (LANES - t_new)
    dist_n = t - tp
    valid_n = (tp >= 0) & (dist_n >= 0) & (dist_n % dil == 0) & (dist_n // dil <= SPAN) & (t < t_new)
    new = np.where(valid_n[None], -slopes * dist_n.astype(np.float32)[None], np.float32(NEG))
    return cache.astype(np.float32), new.astype(np.float32)


def _sample_kernel(*refs, t_new):
    q_refs = refs[0:3]
    c_refs = refs[3:6]
    n_refs = refs[6:9]
    bc_refs = refs[9:12]
    bn_refs = refs[12:15]
    n_in = 15 + (len(refs) - 19)
    o_ref = refs[n_in]
    oc_refs = refs[n_in + 1:n_in + 4]
    b = pl.program_id(1)
    shift = (LANES - t_new) - t_new * b
    lane = lax.broadcasted_iota(jnp.int32, (LANES, LANES), 1)
    heads_per_step = LANES // HEAD_DIM
    per_head = [[] for _ in range(heads_per_step)]
    for g in range(N_GROUPS):
        c_ref = c_refs[g]
        length = c_ref.shape[-1]
        new = [pltpu.roll(n_refs[g][0, kv], shift, axis=1) for kv in range(2)]
        for e in range(heads_per_step):
            rows = slice(e * HEAD_DIM, (e + 1) * HEAD_DIM)
            qh = q_refs[g][0, :, rows].astype(BF16)
            s_c = _dot(qh, c_ref[0, 0, rows, :].astype(BF16)) + bc_refs[g][0, e]
            s_n = _dot(qh, new[0][rows, :].astype(BF16)) + bn_refs[g][0, e]
            m = jnp.maximum(jnp.max(s_c, axis=-1, keepdims=True), jnp.max(s_n, axis=-1, keepdims=True))
            p_c = jnp.exp(s_c - m)
            p_n = jnp.exp(s_n - m)
            l = jnp.sum(p_c, axis=-1, keepdims=True) + jnp.sum(p_n, axis=-1, keepdims=True)
            o = (_dot_nt(p_c.astype(BF16), c_ref[0, 1, rows, :].astype(BF16))
                 + _dot_nt(p_n.astype(BF16), new[1][rows, :].astype(BF16)))
            per_head[e].append((o / l, m + jnp.log(l)))
        for kv in range(2):
            rolled = pltpu.roll(c_ref[0, kv], length - t_new, axis=1)
            if length > LANES:
                oc_refs[g][0, kv, :, 0:length - LANES] = rolled[:, 0:length - LANES]
            oc_refs[g][0, kv, :, length - LANES:length] = jnp.where(
                lane >= LANES - t_new, new[kv], rolled[:, length - LANES:length])
    for e in range(heads_per_step):
        os_, lses = zip(*per_head[e])
        o_ref[0, :, e * HEAD_DIM:(e + 1) * HEAD_DIM] = _combine_groups(list(os_), list(lses))


def _sample_mixer(q8, caches_n, kvt_new, layer, prev_outs, nseq, t_new):
    hsteps = ATT_WIDTH // LANES
    tables = [_sample_bias(g, caches_n[g].shape[-1], t_new) for g in range(N_GROUPS)]
    bc = [jnp.asarray(tc.reshape(hsteps, 2, Q_ROWS, -1)) for tc, _ in tables]
    bn = [jnp.asarray(tn.reshape(hsteps, 2, Q_ROWS, LANES)) for _, tn in tables]
    kvt4 = kvt_new.reshape(N_GROUPS, 2, ATT_WIDTH, nseq * t_new)
    assert nseq * t_new == LANES
    in_specs, args = [], []
    for g in range(N_GROUPS):
        in_specs.append(pl.BlockSpec((1, Q_ROWS, LANES), lambda hc, b, g=g: (b, 0, g * hsteps + hc)))
        args.append(q8)
    cache_specs = []
    for g in range(N_GROUPS):
        length = caches_n[g].shape[-1]
        cache_specs.append(pl.BlockSpec((1, 2, LANES, length),
                                        lambda hc, b: (layer * nseq + b, 0, hc, 0)))
    in_specs += cache_specs
    args += list(caches_n)
    for g in range(N_GROUPS):
        in_specs.append(pl.BlockSpec((1, 2, LANES, LANES), lambda hc, b, g=g: (g, 0, hc, 0)))
        args.append(kvt4)
    for tbl in bc + bn:
        in_specs.append(pl.BlockSpec((1,) + tbl.shape[1:], lambda hc, b: (hc, 0, 0, 0)))
        args.append(tbl)
    aliases = {}
    if prev_outs is not None:
        for g in range(N_GROUPS):
            in_specs.append(pl.BlockSpec(memory_space=pl.ANY))
            aliases[len(args)] = 1 + g
            args.append(prev_outs[g])
    out_shape = ([jax.ShapeDtypeStruct((nseq, Q_ROWS, ATT_WIDTH), F32)]
                 + [jax.ShapeDtypeStruct(c.shape, F32) for c in caches_n])
    out_specs = [pl.BlockSpec((1, Q_ROWS, LANES), lambda hc, b: (b, 0, hc))] + cache_specs
    outs = pl.pallas_call(
        functools.partial(_sample_kernel, t_new=t_new),
        grid=(hsteps, nseq),
        in_specs=in_specs,
        out_specs=out_specs,
        out_shape=out_shape,
        input_output_aliases=aliases,
        compiler_params=_params(2),
        name="sample_mixer",
    )(*args)
    return outs[0], outs[1:]


def _native_view(cache):
    depth, nseq, length = cache.shape[:3]
    return cache.transpose(0, 1, 3, 4, 5, 2).reshape(depth * nseq, 2, ATT_WIDTH, length)


def _from_native(x, depth, nseq):
    length = x.shape[-1]
    return x.reshape(depth, nseq, 2, N_SLOTS, HEAD_DIM, length).transpose(0, 1, 5, 2, 3, 4)


def kernel(x_prompt, x_sample, c_prompt, c_sample, cache_kv_g0, cache_kv_g1, cache_kv_g2, state_conv, w_ada, b_ada, w_in, b_in, w_dw, b_dw, conv_ln_g, conv_ln_b, w_pc, b_pc, w_pa, b_pa, w_out, b_out, ln1_g, ln1_b, w_gate, w_up, w_down, ln2_g, ln2_b):
    depth = w_in.shape[0]
    bsz, seq, d = x_prompt.shape
    nseq, t_new, _ = x_sample.shape
    alpha = (2 * depth) ** 0.25
    tm = 512
    m_s = nseq * t_new

    mod_all = _adaln(jnp.concatenate([c_prompt, c_sample], axis=0), w_ada, b_ada)
    caches_n = [_native_view(c) for c in (cache_kv_g0, cache_kv_g1, cache_kv_g2)]
    state_n = state_conv.transpose(0, 2, 1, 3)

    xp = x_prompt
    xs = x_sample.transpose(1, 0, 2).reshape(1, m_s, d)
    row = lambda v: v.reshape(1, -1)
    kv_prompt = [[] for _ in range(N_GROUPS)]
    conv_prompt, conv_sample = [], []
    cache_outs = None
    for l in range(depth):
        wi, bi = w_in[l], b_in[l]
        q_cols = slice(QKV_OFF, QKV_OFF + N_GROUPS * ATT_WIDTH)
        w_a = jnp.concatenate([wi[:, :QKV_OFF], wi[:, GATE_OFF:], wi[:, q_cols]], axis=1).astype(BF16)
        b_a = row(jnp.concatenate([bi[:QKV_OFF], bi[GATE_OFF:], bi[q_cols]]))
        kv_cols = []
        for g in range(N_GROUPS):
            for which in (1, 2):
                c0 = QKV_OFF + (which * N_GROUPS + g) * ATT_WIDTH
                kv_cols.append(slice(c0, c0 + ATT_WIDTH))
        w_b32 = jnp.concatenate([wi[:, c] for c in kv_cols], axis=1)
        b_b = row(jnp.concatenate([bi[c] for c in kv_cols]))
        w_b = w_b32.astype(BF16)
        w_bt = w_b32.T.astype(BF16)
        b_bt = b_b.reshape(-1, 1)
        wts_a = [w_dw[l], row(b_dw[l]), row(conv_ln_g[l]), row(conv_ln_b[l]),
                 w_pc[l].astype(BF16), row(b_pc[l]), w_pa[l].astype(BF16), row(b_pa[l]),
                 w_out[l].astype(BF16), row(b_out[l]), row(ln1_g[l]), row(ln1_b[l])]
        wts_b = [w_gate[l].astype(BF16), w_up[l].astype(BF16), w_down[l].astype(BF16),
                 row(ln2_g[l]), row(ln2_b[l])]
        mod_p = mod_all[l, :bsz].reshape(bsz, 1, MOD_WIDTH)
        mod_s = jnp.tile(mod_all[l, bsz:], (t_new, 1)).reshape(1, m_s, MOD_WIDTH)
        mod_sq = jnp.repeat(mod_all[l, bsz:], t_new, axis=0).reshape(1, m_s, MOD_WIDTH)

        glu, gate_c, gate_a, *qx = _inproj_a(xp, mod_p, w_a, b_a, tm)
        *kvb, n0, n1, n2 = _inproj_b(xp, mod_p, w_b, b_b, w_bt, b_bt, tm)
        for g, nat in enumerate((n0, n1, n2)):
            kv_prompt[g].append(nat)
        conv_prompt.append(glu[:, seq - (CONV_WIDTH - 1):, :])
        os_, lses = [], []
        for g in range(N_GROUPS):
            o, lse = _prompt_attention(qx[g], kvb[g], DILATIONS[g])
            os_.append(o)
            lses.append(lse)
        x1 = _post_a(xp, mod_p, glu, os_, lses, gate_c, gate_a, wts_a, tm, alpha)
        xp = _post_b(x1, mod_p, wts_b, tm, alpha)

        xs_q = xs.reshape(t_new, nseq, d).transpose(1, 0, 2).reshape(1, m_s, d)
        glu_s, gc_s, ga_s, q_s, kvt_new = _inproj_s(xs, mod_s, xs_q, mod_sq, w_a, b_a, w_bt, b_bt)
        q8 = jnp.pad(q_s.reshape(t_new, nseq, N_GROUPS * ATT_WIDTH).transpose(1, 0, 2),
                     ((0, 0), (0, Q_ROWS - t_new), (0, 0)))
        o8, cache_outs = _sample_mixer(q8, caches_n, kvt_new, l, cache_outs, nseq, t_new)
        o_s = o8[:, :t_new].transpose(1, 0, 2).reshape(1, m_s, ATT_WIDTH)
        xs, nconv = _post_s(xs, mod_s, state_n, l, glu_s, o_s, gc_s, ga_s, wts_a + wts_b,
                            nseq, t_new, alpha)
        conv_sample.append(nconv[0])

    kv_p = [_from_native(jnp.stack(kv_prompt[g]).reshape(depth * bsz, 2, ATT_WIDTH, -1), depth, bsz)
            for g in range(N_GROUPS)]
    kv_s = [_from_native(cache_outs[g], depth, nseq) for g in range(N_GROUPS)]
    return (xp, xs.reshape(t_new, nseq, d).transpose(1, 0, 2),
            kv_p[0], kv_p[1], kv_p[2], jnp.stack(conv_prompt),
            kv_s[0], kv_s[1], kv_s[2], jnp.stack(conv_sample).transpose(0, 2, 1, 3))
```

```python
import functools

import numpy as np
import jax
import jax.numpy as jnp
from jax import lax
from jax.experimental import pallas as pl
from jax.experimental.pallas import tpu as pltpu

D_MODEL = 1024
CONV_CH = 512
CONV_WIDTH = 31
N_SLOTS = 8
HEAD_DIM = 64
ATT_WIDTH = N_SLOTS * HEAD_DIM
WINDOWS = (128, 512, 2048)
DILATIONS = (1, 4, 16)
N_GROUPS = 3
SPAN = 128
BLOCK = 128
LN_EPS = 1e-5
QKV_OFF = 2 * CONV_CH
GATE_OFF = QKV_OFF + 3 * N_GROUPS * ATT_WIDTH
IN_WIDTH = GATE_OFF + 2 * D_MODEL
KV_WIDTH = 2 * ATT_WIDTH
MOD_WIDTH = 6 * D_MODEL
LANES = 128
NEG = -1e30
Q_ROWS = 8
CONV_HALO = 32
VMEM_LIMIT = 56 * 1024 * 1024

F32 = jnp.float32
BF16 = jnp.bfloat16
GATE_DTYPE = jnp.bfloat16

_NT = (((1,), (1,)), ((), ()))


def _q_col(g):
    return QKV_OFF + g * ATT_WIDTH


def _k_col(g):
    return QKV_OFF + (N_GROUPS + g) * ATT_WIDTH


def _v_col(g):
    return QKV_OFF + (2 * N_GROUPS + g) * ATT_WIDTH


def _dot(a, b):
    return jnp.dot(a, b, preferred_element_type=F32)


def _dot_nt(a, b):
    return lax.dot_general(a, b, _NT, preferred_element_type=F32)


def _sigmoid(x):
    return 1.0 / (1.0 + jnp.exp(-x))


def _silu(x):
    return x * _sigmoid(x)


def _layer_norm(x, g, b):
    mu = jnp.mean(x, axis=-1, keepdims=True)
    xc = x - mu
    var = jnp.mean(xc * xc, axis=-1, keepdims=True)
    return xc * lax.rsqrt(var + LN_EPS) * g + b


def _mod_slice(m, i):
    return m[:, i * D_MODEL:(i + 1) * D_MODEL]


def _const_spec(shape):
    nd = len(shape)
    return pl.BlockSpec(shape, lambda *_: (0,) * nd, pipeline_mode=pl.Buffered(1))


def _layer_spec(arr, layer):
    nd = arr.ndim
    return pl.BlockSpec((1,) + arr.shape[1:], lambda *_: (layer,) + (0,) * (nd - 1),
                        pipeline_mode=pl.Buffered(1))


def _params(n_axes):
    return pltpu.CompilerParams(dimension_semantics=("arbitrary",) * n_axes,
                                vmem_limit_bytes=VMEM_LIMIT)


def _adaln_kernel(c_ref, w_ref, b_ref, o_ref):
    c = c_ref[...]
    s = _silu(c).astype(BF16)
    o_ref[0] = _dot(s, w_ref[0].astype(BF16)) + b_ref[0]


def _adaln(c_all, w_ada, b_ada):
    depth, d, width = w_ada.shape
    rows = c_all.shape[0]
    tn = 1024
    return pl.pallas_call(
        _adaln_kernel,
        grid=(depth, width // tn),
        in_specs=[pl.BlockSpec((rows, d), lambda l, j: (0, 0)),
                  pl.BlockSpec((1, d, tn), lambda l, j: (l, 0, j)),
                  pl.BlockSpec((1, 1, tn), lambda l, j: (l, 0, j))],
        out_specs=pl.BlockSpec((1, rows, tn), lambda l, j: (l, 0, j)),
        out_shape=jax.ShapeDtypeStruct((depth, rows, width), F32),
        compiler_params=_params(2),
        name="adaln",
    )(c_all, w_ada, b_ada.reshape(depth, 1, width))


def _modulated(x_ref, mod_ref, shift_i, scale_i):
    m = mod_ref[0]
    x = x_ref[0]
    return (x * (1.0 + _mod_slice(m, scale_i)) + _mod_slice(m, shift_i)).astype(BF16)


def _proj(u, w_ref, b_ref, c0, width):
    return _dot(u, w_ref[0, :, c0:c0 + width]) + b_ref[0, :, c0:c0 + width]


def _glu_gates_q(u, w_ref, b_ref):
    glu = _proj(u, w_ref, b_ref, 0, CONV_CH) * _sigmoid(_proj(u, w_ref, b_ref, CONV_CH, CONV_CH))
    gate_c = _sigmoid(_proj(u, w_ref, b_ref, GATE_OFF, D_MODEL))
    gate_a = _sigmoid(_proj(u, w_ref, b_ref, GATE_OFF + D_MODEL, D_MODEL))
    qs = [_proj(u, w_ref, b_ref, _q_col(g), ATT_WIDTH) * (HEAD_DIM ** -0.5) for g in range(N_GROUPS)]
    return glu, gate_c, gate_a, qs


def _residue_rows(t, scr_ref, slot, dil):
    if dil == 1:
        return [t]
    rows = t.shape[0]
    scr_ref[slot] = t
    return [scr_ref[slot, pl.ds(r, rows // dil, stride=dil), :] for r in range(dil)]


def _inproj_a_kernel(x_ref, mod_ref, w_ref, b_ref, glu_ref, gc_ref, ga_ref, q0_ref, q1_ref, q2_ref,
                     scr_ref):
    u = _modulated(x_ref, mod_ref, 0, 1)
    glu, gate_c, gate_a, qs = _glu_gates_q(u, w_ref, b_ref)
    glu_ref[0] = glu
    gc_ref[0] = gate_c.astype(gc_ref.dtype)
    ga_ref[0] = gate_a.astype(ga_ref.dtype)
    for g, (q, q_ref) in enumerate(zip(qs, (q0_ref, q1_ref, q2_ref))):
        dil = DILATIONS[g]
        for p in range(ATT_WIDTH // LANES):
            parts = _residue_rows(q[:, p * LANES:(p + 1) * LANES], scr_ref, p, dil)
            for r, t in enumerate(parts):
                lane = lax.broadcasted_iota(jnp.int32, t.shape, 1)
                q_ref[0, r, :, (2 * p) * LANES:(2 * p + 1) * LANES] = jnp.where(lane < HEAD_DIM, t, 0.0).astype(BF16)
                q_ref[0, r, :, (2 * p + 1) * LANES:(2 * p + 2) * LANES] = jnp.where(lane >= HEAD_DIM, t, 0.0).astype(BF16)


def _residue_spec(tm, dil, width):
    return pl.BlockSpec((1, dil, tm // dil, width), lambda b, i: (b, 0, i, 0))


def _inproj_a(x, mod, w_in, b_in, layer, tm):
    bsz, s, d = x.shape
    nt = s // tm
    tile = lambda width: pl.BlockSpec((1, tm, width), lambda b, i: (b, i, 0))
    out_shape = ([jax.ShapeDtypeStruct((bsz, s, CONV_CH), F32),
                  jax.ShapeDtypeStruct((bsz, s, d), GATE_DTYPE),
                  jax.ShapeDtypeStruct((bsz, s, d), GATE_DTYPE)]
                 + [jax.ShapeDtypeStruct((bsz, dil, s // dil, 2 * ATT_WIDTH), BF16) for dil in DILATIONS])
    return pl.pallas_call(
        _inproj_a_kernel,
        grid=(bsz, nt),
        in_specs=[tile(d),
                  pl.BlockSpec((1, 1, MOD_WIDTH), lambda b, i: (b, 0, 0)),
                  _layer_spec(w_in, layer), _layer_spec(b_in, layer)],
        out_specs=[tile(CONV_CH), tile(d), tile(d)]
                  + [_residue_spec(tm, dil, 2 * ATT_WIDTH) for dil in DILATIONS],
        out_shape=out_shape,
        scratch_shapes=[pltpu.VMEM((ATT_WIDTH // LANES, tm, LANES), F32)],
        compiler_params=_params(2),
        name="inproj_a",
    )(x, mod, w_in, b_in)


def _native_kv(wt_ref, bt_ref, g, u):
    rows = slice(g * KV_WIDTH, (g + 1) * KV_WIDTH)
    return _dot_nt(wt_ref[rows, :], u) + bt_ref[rows, :]


def _conv_ln_silu(win_ref, rows, wdw_ref, bdw_ref, cg_ref, cb_ref, out_ref, rb):
    off = CONV_HALO - (CONV_WIDTH - 1)
    nslab = CONV_CH // LANES
    for r0 in range(0, rows, rb):
        accs = []
        for c in range(nslab):
            cols = slice(c * LANES, (c + 1) * LANES)
            acc = jnp.zeros((rb, LANES), F32) + bdw_ref[0, :, cols]
            for j in range(CONV_WIDTH):
                acc = acc + win_ref[c, pl.ds(r0 + off + j, rb, stride=1), :] * wdw_ref[0, j:j + 1, cols]
            accs.append(acc)
        y = _silu(_layer_norm(jnp.concatenate(accs, axis=1), cg_ref[0], cb_ref[0]))
        out_ref[0, r0:r0 + rb, :] = y.astype(out_ref.dtype)


def _inproj_b_kernel(*refs, tm, seq, aliased):
    (x_ref, mod_ref, w_ref, b_ref, wt_ref, bt_ref, glu_ref, halo_ref,
     wdw_ref, bdw_ref, cg_ref, cb_ref) = refs[:12]
    n_in = 12 + (N_GROUPS if aliased else 0)
    kv_refs = refs[n_in:n_in + N_GROUPS]
    nat_refs = refs[n_in + N_GROUPS:n_in + 2 * N_GROUPS]
    ycv_ref, scr_ref, win_ref = refs[n_in + 2 * N_GROUPS:]
    ti = pl.program_id(1)
    halo = halo_ref[0]
    halo = jnp.where(ti > 0, halo, jnp.zeros_like(halo))
    glu = glu_ref[0]
    for c in range(CONV_CH // LANES):
        win_ref[c, 0:CONV_HALO, :] = halo[:, c * LANES:(c + 1) * LANES]
        win_ref[c, CONV_HALO:CONV_HALO + tm, :] = glu[:, c * LANES:(c + 1) * LANES]
    _conv_ln_silu(win_ref, tm, wdw_ref, bdw_ref, cg_ref, cb_ref, ycv_ref, rb=64)
    u = _modulated(x_ref, mod_ref, 0, 1)
    for g, kv_ref in enumerate(kv_refs):
        for half, c0 in enumerate((_k_col(g), _v_col(g))):
            t = _proj(u, w_ref, b_ref, c0, ATT_WIDTH)
            for p in range(ATT_WIDTH // LANES):
                parts = _residue_rows(t[:, p * LANES:(p + 1) * LANES], scr_ref, half * 4 + p, DILATIONS[g])
                for r, part in enumerate(parts):
                    cols = slice(half * ATT_WIDTH + p * LANES, half * ATT_WIDTH + (p + 1) * LANES)
                    kv_ref[0, r, :, cols] = part.astype(BF16)
    for g, n_ref in reversed(list(enumerate(nat_refs))):
        win = min(WINDOWS[g], seq)
        if win >= tm:
            first = (seq - win) // tm
            if first == 0:
                n_ref[0] = _native_kv(wt_ref, bt_ref, g, u)
            else:
                @pl.when(ti >= first)
                def _(g=g, n_ref=n_ref):
                    n_ref[0] = _native_kv(wt_ref, bt_ref, g, u)
        else:
            @pl.when(ti == seq // tm - 1)
            def _(g=g, n_ref=n_ref, win=win):
                n_ref[0] = _native_kv(wt_ref, bt_ref, g, u[tm - win:, :])


def _inproj_b(x, mod, w_in, b_in, w_bt, b_bt, glu, conv_w, layer, depth, prev_nat, tm):
    bsz, s, d = x.shape
    nt = s // tm
    assert tm % CONV_HALO == 0
    tile = lambda width: pl.BlockSpec((1, tm, width), lambda b, i: (b, i, 0))
    halo_spec = pl.BlockSpec(
        (1, CONV_HALO, CONV_CH), lambda b, i: (b, jnp.maximum(i * (tm // CONV_HALO) - 1, 0), 0))
    nat_specs, nat_shapes = [], []
    for g in range(N_GROUPS):
        win = min(WINDOWS[g], s)
        assert win % tm == 0 or tm % win == 0
        nat_shapes.append(jax.ShapeDtypeStruct((depth * bsz, KV_WIDTH, win), F32))
        if win >= tm:
            first = (s - win) // tm
            nat_specs.append(pl.BlockSpec(
                (1, KV_WIDTH, tm),
                lambda b, i, first=first: (layer * bsz + b, 0, jnp.maximum(i - first, 0))))
        else:
            nat_specs.append(pl.BlockSpec((1, KV_WIDTH, win), lambda b, i: (layer * bsz + b, 0, 0)))
    args = [x, mod, w_in, b_in, w_bt, b_bt, glu, glu, *conv_w]
    in_specs = [tile(d), pl.BlockSpec((1, 1, MOD_WIDTH), lambda b, i: (b, 0, 0)),
                _layer_spec(w_in, layer), _layer_spec(b_in, layer),
                _const_spec(w_bt.shape), _const_spec(b_bt.shape),
                tile(CONV_CH), halo_spec] + [_layer_spec(w, layer) for w in conv_w]
    aliases = {}
    if prev_nat is not None:
        for g in range(N_GROUPS):
            in_specs.append(pl.BlockSpec(memory_space=pl.ANY))
            aliases[len(args)] = N_GROUPS + g
            args.append(prev_nat[g])
    outs = pl.pallas_call(
        functools.partial(_inproj_b_kernel, tm=tm, seq=s, aliased=prev_nat is not None),
        grid=(bsz, nt),
        in_specs=in_specs,
        out_specs=[_residue_spec(tm, dil, KV_WIDTH) for dil in DILATIONS] + nat_specs + [tile(CONV_CH)],
        out_shape=[jax.ShapeDtypeStruct((bsz, dil, s // dil, KV_WIDTH), BF16) for dil in DILATIONS]
                  + nat_shapes + [jax.ShapeDtypeStruct((bsz, s, CONV_CH), BF16)],
        scratch_shapes=[pltpu.VMEM((KV_WIDTH // LANES, tm, LANES), F32),
                        pltpu.VMEM((CONV_CH // LANES, CONV_HALO + tm, LANES), F32)],
        input_output_aliases=aliases,
        compiler_params=_params(2),
        name="inproj_b",
    )(*args)
    return outs[:N_GROUPS], outs[N_GROUPS:2 * N_GROUPS], outs[2 * N_GROUPS]


def _inproj_s_kernel(x_ref, mod_ref, xq_ref, modq_ref, w_ref, b_ref, wt_ref, bt_ref,
                     glu_ref, gc_ref, ga_ref, q_ref, kvt_ref):
    u = _modulated(x_ref, mod_ref, 0, 1)
    glu, gate_c, gate_a, qs = _glu_gates_q(u, w_ref, b_ref)
    glu_ref[0] = glu
    gc_ref[0] = gate_c.astype(gc_ref.dtype)
    ga_ref[0] = gate_a.astype(ga_ref.dtype)
    for g, q in enumerate(qs):
        q_ref[0, :, g * ATT_WIDTH:(g + 1) * ATT_WIDTH] = q
    uq = _modulated(xq_ref, modq_ref, 0, 1)
    for g in range(N_GROUPS):
        kvt_ref[g * KV_WIDTH:(g + 1) * KV_WIDTH, :] = _native_kv(wt_ref, bt_ref, g, uq)


def _inproj_s(x, mod, xq, modq, w_in, b_in, w_bt, b_bt, layer):
    _, m, d = x.shape
    full = lambda *shape: _const_spec(shape)
    whole = lambda *shape: pl.BlockSpec(shape, lambda i: (0,) * len(shape))
    return pl.pallas_call(
        _inproj_s_kernel,
        in_specs=[full(1, m, d), full(1, m, MOD_WIDTH), full(1, m, d), full(1, m, MOD_WIDTH),
                  _layer_spec(w_in, layer), _layer_spec(b_in, layer),
                  full(*w_bt.shape), full(*b_bt.shape)],
        out_specs=[whole(1, m, CONV_CH), whole(1, m, d), whole(1, m, d),
                   whole(1, m, N_GROUPS * ATT_WIDTH), whole(N_GROUPS * KV_WIDTH, m)],
        out_shape=[jax.ShapeDtypeStruct((1, m, CONV_CH), F32),
                   jax.ShapeDtypeStruct((1, m, d), GATE_DTYPE),
                   jax.ShapeDtypeStruct((1, m, d), GATE_DTYPE),
                   jax.ShapeDtypeStruct((1, m, N_GROUPS * ATT_WIDTH), F32),
                   jax.ShapeDtypeStruct((N_GROUPS * KV_WIDTH, m), F32)],
        grid=(1,),
        compiler_params=_params(1),
        name="inproj_s",
    )(x, mod, xq, modq, w_in, b_in, w_bt, b_bt)


def _alibi_slopes():
    return (2.0 ** (-8.0 * (np.arange(N_SLOTS) + 1) / N_SLOTS)).astype(np.float32)


def _prompt_bias(dil, key_blocks):
    qi = np.arange(BLOCK)[:, None]
    kj = np.arange(key_blocks * BLOCK)[None, :]
    tables = []
    for lead in (0, key_blocks - 1):
        rel = lead * BLOCK + qi - kj
        valid = (rel >= 0) & (rel <= SPAN)
        bias = -_alibi_slopes()[:, None, None] * (rel * dil).astype(np.float32)[None]
        tables.append(np.where(valid[None], bias, np.float32(NEG)))
    return np.stack(tables).astype(np.float32)


def _attn_kernel(q_ref, kv_ref, bias_ref, o_ref, lse_ref, s_scr, m_scr, *, nb, n_units):
    kw = bias_ref.shape[-1]
    lane = lax.broadcasted_iota(jnp.int32, (BLOCK, LANES), 1)
    pick = lambda a, b: jnp.where(lane < HEAD_DIM, a, b)

    def place(u):
        if isinstance(u, int):
            rr, c = divmod(u, nb)
            r0 = c * BLOCK
            k0 = max(r0 - (kw - BLOCK), 0)
            return rr, r0, k0, (0 if c == 0 else 1)
        rr = u // nb
        c = u % nb
        r0 = pl.multiple_of(c * BLOCK, BLOCK)
        k0 = pl.multiple_of(jnp.maximum(r0 - (kw - BLOCK), 0), BLOCK)
        return rr, r0, k0, jnp.minimum(c, 1)

    ones = jnp.ones((kw, LANES), BF16)

    def scores(u, slot):
        rr, r0, k0, tbl = place(u)
        for p in range(ATT_WIDTH // LANES):
            kp = kv_ref[0, rr, pl.ds(k0, kw), p * LANES:(p + 1) * LANES]
            q2 = q_ref[0, rr, pl.ds(r0, BLOCK), 2 * p * LANES:(2 * p + 2) * LANES]
            q2 = jnp.concatenate([q2[:, :LANES], q2[:, LANES:]], axis=0)
            bias2 = jnp.concatenate([bias_ref[tbl, 2 * p], bias_ref[tbl, 2 * p + 1]], axis=0)
            s = _dot_nt(q2, kp) + bias2
            s_scr[slot, p] = s
            m_scr[slot, p] = jnp.max(s, axis=-1, keepdims=True)

    def finish(u, slot):
        rr, r0, k0, _ = place(u)
        for p in range(ATT_WIDTH // LANES):
            vp = kv_ref[0, rr, pl.ds(k0, kw), ATT_WIDTH + p * LANES:ATT_WIDTH + (p + 1) * LANES]
            m = m_scr[slot, p]
            pexp = jnp.exp(s_scr[slot, p] - m).astype(BF16)
            pv = _dot(pexp, jnp.concatenate([vp, ones], axis=1))
            l2 = pick(pv[:BLOCK, LANES:], pv[BLOCK:, LANES:])
            cols = slice(p * LANES, (p + 1) * LANES)
            o_ref[0, rr, pl.ds(r0, BLOCK), cols] = pick(pv[:BLOCK, :LANES], pv[BLOCK:, :LANES]) / l2
            lse_ref[0, rr, pl.ds(r0, BLOCK), cols] = pick(m[:BLOCK], m[BLOCK:]) + jnp.log(l2)

    scores(0, 0)

    def body(u, carry):
        slot = u % 2
        finish(u, slot)
        scores(u + 1, 1 - slot)
        return carry
    lax.fori_loop(0, n_units - 1, body, 0)
    finish(n_units - 1, (n_units - 1) % 2)


def _prompt_attention(qx, kvb, dil):
    bsz, _, n, _ = qx.shape
    assert n % BLOCK == 0
    nb = n // BLOCK
    nres = max(1, min(dil, 16 // nb))
    assert dil % nres == 0
    bias = jnp.asarray(_prompt_bias(dil, min(nb, 2)))
    sub = lambda width: pl.BlockSpec((1, nres, n, width), lambda b, r: (b, r, 0, 0))
    return pl.pallas_call(
        functools.partial(_attn_kernel, nb=nb, n_units=nres * nb),
        grid=(bsz, dil // nres),
        in_specs=[sub(2 * ATT_WIDTH), sub(KV_WIDTH), _const_spec(bias.shape)],
        out_specs=[sub(ATT_WIDTH), sub(ATT_WIDTH)],
        out_shape=[jax.ShapeDtypeStruct((bsz, dil, n, ATT_WIDTH), F32)] * 2,
        scratch_shapes=[pltpu.VMEM((2, N_SLOTS // 2, 2 * BLOCK, bias.shape[-1]), F32),
                        pltpu.VMEM((2, N_SLOTS // 2, 2 * BLOCK, 1), F32)],
        compiler_params=_params(2),
        name=f"attn_d{dil}",
    )(qx, kvb, bias)


def _combine_groups(os_, lses):
    m = functools.reduce(jnp.maximum, lses)
    ws = [jnp.exp(l - m) for l in lses]
    den = functools.reduce(lambda a, b: a + b, ws)
    num = functools.reduce(lambda a, b: a + b, [w * o for w, o in zip(ws, os_)])
    return num / den


def _merge_to_ln1(x, m, ybuf, o_att, gate_c, gate_a,
                  wpc_ref, bpc_ref, wpa_ref, bpa_ref, wout_ref, bout_ref, g1_ref, b1_ref, alpha):
    y_conv = _dot(ybuf, wpc_ref[0]) + bpc_ref[0]
    y_att = _dot(o_att.astype(BF16), wpa_ref[0]) + bpa_ref[0]
    y = (gate_c * y_conv + gate_a * y_att).astype(BF16)
    y = _dot(y, wout_ref[0]) + bout_ref[0]
    return _layer_norm(alpha * x + (1.0 + _mod_slice(m, 2)) * y, g1_ref[0], b1_ref[0])


def _ffn_to_ln2(x1, m, wg_ref, wu_ref, wd_ref, g2_ref, b2_ref, alpha, chunk):
    u2 = (x1 * (1.0 + _mod_slice(m, 4)) + _mod_slice(m, 3)).astype(BF16)
    d_ff = wg_ref.shape[-1]
    h = None
    for c0 in range(0, d_ff, chunk):
        c1 = min(c0 + chunk, d_ff)
        a = (_silu(_dot(u2, wg_ref[0, :, c0:c1])) * _dot(u2, wu_ref[0, :, c0:c1])).astype(BF16)
        part = _dot(a, wd_ref[0, c0:c1, :])
        h = part if h is None else h + part
    return _layer_norm(alpha * x1 + (1.0 + _mod_slice(m, 5)) * h, g2_ref[0], b2_ref[0])


def _token_order(ref, scr_ref, slot, p, dil):
    cols = slice(p * LANES, (p + 1) * LANES)
    if dil == 1:
        return ref[0, 0, :, cols]
    n = ref.shape[2]
    for r in range(dil):
        scr_ref[slot, pl.ds(r, n, stride=dil), :] = ref[0, r, :, cols]
    return scr_ref[slot]


def _post_a_kernel(x_ref, mod_ref, ycv_ref, o0_ref, o1_ref, o2_ref, l0_ref, l1_ref, l2_ref,
                   gc_ref, ga_ref, wpc_ref, bpc_ref, wpa_ref, bpa_ref, wout_ref, bout_ref,
                   g1_ref, b1_ref, out_ref, scr_ref, *, alpha):
    o_refs = (o0_ref, o1_ref, o2_ref)
    l_refs = (l0_ref, l1_ref, l2_ref)
    chunks = []
    for p in range(ATT_WIDTH // LANES):
        os_ = [_token_order(o_refs[g], scr_ref, 2 * g, p, DILATIONS[g]) for g in range(N_GROUPS)]
        lses = [_token_order(l_refs[g], scr_ref, 2 * g + 1, p, DILATIONS[g]) for g in range(N_GROUPS)]
        chunks.append(_combine_groups(os_, lses).astype(BF16))
    o_att = jnp.concatenate(chunks, axis=1)
    out_ref[0] = _merge_to_ln1(x_ref[0], mod_ref[0], ycv_ref[0], o_att,
                               gc_ref[0].astype(F32), ga_ref[0].astype(F32),
                               wpc_ref, bpc_ref, wpa_ref, bpa_ref, wout_ref, bout_ref,
                               g1_ref, b1_ref, alpha)


def _post_a(x, mod, ycv, os_, lses, gate_c, gate_a, wts, layer, tm, alpha):
    bsz, s, d = x.shape
    nt = s // tm
    tile = lambda width: pl.BlockSpec((1, tm, width), lambda b, i: (b, i, 0))
    res_specs = [_residue_spec(tm, dil, ATT_WIDTH) for dil in DILATIONS]
    return pl.pallas_call(
        functools.partial(_post_a_kernel, alpha=alpha),
        grid=(bsz, nt),
        in_specs=([tile(d), pl.BlockSpec((1, 1, MOD_WIDTH), lambda b, i: (b, 0, 0)), tile(CONV_CH)]
                  + res_specs + res_specs + [tile(d), tile(d)]
                  + [_layer_spec(w, layer) for w in wts]),
        out_specs=tile(d),
        out_shape=jax.ShapeDtypeStruct((bsz, s, d), F32),
        scratch_shapes=[pltpu.VMEM((2 * N_GROUPS, tm, LANES), F32)],
        compiler_params=_params(2),
        name="post_a",
    )(x, mod, ycv, *os_, *lses, gate_c, gate_a, *wts)


def _post_b_kernel(x_ref, mod_ref, wg_ref, wu_ref, wd_ref, g2_ref, b2_ref, out_ref, *, alpha, chunk):
    out_ref[0] = _ffn_to_ln2(x_ref[0], mod_ref[0], wg_ref, wu_ref, wd_ref, g2_ref, b2_ref, alpha, chunk)


def _post_b(x1, mod, wts, layer, tm, alpha):
    bsz, s, d = x1.shape
    tile = pl.BlockSpec((1, tm, d), lambda b, i: (b, i, 0))
    return pl.pallas_call(
        functools.partial(_post_b_kernel, alpha=alpha, chunk=1024),
        grid=(bsz, s // tm),
        in_specs=[tile, pl.BlockSpec((1, 1, MOD_WIDTH), lambda b, i: (b, 0, 0))]
                 + [_layer_spec(w, layer) for w in wts],
        out_specs=tile,
        out_shape=jax.ShapeDtypeStruct((bsz, s, d), F32),
        compiler_params=_params(2),
        name="post_b",
    )(x1, mod, *wts)


def _post_s_kernel(x_ref, mod_ref, state_ref, glu_ref, o_ref, gc_ref, ga_ref,
                   wdw_ref, bdw_ref, cg_ref, cb_ref,
                   wpc_ref, bpc_ref, wpa_ref, bpa_ref, wout_ref, bout_ref, g1_ref, b1_ref,
                   wg_ref, wu_ref, wd_ref, g2_ref, b2_ref,
                   out_ref, nconv_ref, ypre_ref, *, nseq, t_new, alpha, chunk):
    ctx = CONV_WIDTH - 1

    def slab(i):
        if i < ctx:
            return state_ref[0, i]
        return glu_ref[0, (i - ctx) * nseq:(i - ctx + 1) * nseq, :]

    for t in range(t_new):
        acc = jnp.zeros((nseq, CONV_CH), F32) + bdw_ref[0]
        for j in range(CONV_WIDTH):
            acc = acc + slab(t + j) * wdw_ref[0, j:j + 1, :]
        ypre_ref[t * nseq:(t + 1) * nseq, :] = acc
    for i in range(ctx):
        nconv_ref[0, i] = slab(i + t_new)
    ybuf = _silu(_layer_norm(ypre_ref[...], cg_ref[0], cb_ref[0])).astype(BF16)
    m = mod_ref[0]
    x1 = _merge_to_ln1(x_ref[0], m, ybuf, o_ref[0], gc_ref[0].astype(F32), ga_ref[0].astype(F32),
                       wpc_ref, bpc_ref, wpa_ref, bpa_ref, wout_ref, bout_ref, g1_ref, b1_ref, alpha)
    out_ref[0] = _ffn_to_ln2(x1, m, wg_ref, wu_ref, wd_ref, g2_ref, b2_ref, alpha, chunk)


def _post_s(x, mod, state_n, layer, glu, o_att, gate_c, gate_a, wts, nseq, t_new, alpha):
    _, m, d = x.shape
    ctx = CONV_WIDTH - 1
    full = lambda *shape: _const_spec(shape)
    whole = lambda *shape: pl.BlockSpec(shape, lambda i: (0,) * len(shape))
    return pl.pallas_call(
        functools.partial(_post_s_kernel, nseq=nseq, t_new=t_new, alpha=alpha, chunk=1024),
        in_specs=[full(1, m, d), full(1, m, MOD_WIDTH), _layer_spec(state_n, layer), full(1, m, CONV_CH),
                  full(1, m, ATT_WIDTH), full(1, m, d), full(1, m, d)]
                 + [_layer_spec(w, layer) for w in wts],
        out_specs=[whole(1, m, d), whole(1, ctx, nseq, CONV_CH)],
        out_shape=[jax.ShapeDtypeStruct((1, m, d), F32),
                   jax.ShapeDtypeStruct((1, ctx, nseq, CONV_CH), F32)],
        scratch_shapes=[pltpu.VMEM((m, CONV_CH), F32)],
        grid=(1,),
        compiler_params=_params(1),
        name="post_s",
    )(x, mod, state_n, glu, o_att, gate_c, gate_a, *wts)


def _sample_bias(g, length, t_new):
    dil = DILATIONS[g]
    slopes = _alibi_slopes()[:, None, None]
    t = np.arange(Q_ROWS)[:, None]
    p = np.arange(length)[None, :]
    dist = length + t - p
    valid = (dist % dil == 0) & (dist // dil >= 1) & (dist // dil <= SPAN) & (t < t_new)
    cache = np.where(valid[None], -slopes * dist.astype(np.float32)[None], np.float32(NEG))
    tp = np.arange(LANES)[None, :] - (LANES - t_new)
    dist_n = t - tp
    valid_n = (tp >= 0) & (dist_n >= 0) & (dist_n % dil == 0) & (dist_n // dil <= SPAN) & (t < t_new)
    new = np.where(valid_n[None], -slopes * dist_n.astype(np.float32)[None], np.float32(NEG))
    return cache.astype(np.float32), new.astype(np.float32)


def _sample_kernel(*refs, t_new):
    q_refs = refs[0:3]
    c_refs = refs[3:6]
    n_refs = refs[6:9]
    bc_refs = refs[9:12]
    bn_refs = refs[12:15]
    n_in = len(refs) - 4
    o_ref = refs[n_in]
    oc_refs = refs[n_in + 1:n_in + 4]
    b = pl.program_id(1)
    shift = (LANES - t_new) - t_new * b
    lane = lax.broadcasted_iota(jnp.int32, (LANES, LANES), 1)
    heads_per_step = LANES // HEAD_DIM
    per_head = [[] for _ in range(heads_per_step)]
    for g in range(N_GROUPS):
        c_ref = c_refs[g]
        length = c_ref.shape[-1]
        new = [pltpu.roll(n_refs[g][0, kv], shift, axis=1) for kv in range(2)]
        for e in range(heads_per_step):
            rows = slice(e * HEAD_DIM, (e + 1) * HEAD_DIM)
            qh = q_refs[g][0, :, rows].astype(BF16)
            s_c = _dot(qh, c_ref[0, 0, rows, :].astype(BF16)) + bc_refs[g][0, e]
            s_n = _dot(qh, new[0][rows, :].astype(BF16)) + bn_refs[g][0, e]
            m = jnp.maximum(jnp.max(s_c, axis=-1, keepdims=True), jnp.max(s_n, axis=-1, keepdims=True))
            p_c = jnp.exp(s_c - m)
            p_n = jnp.exp(s_n - m)
            l = jnp.sum(p_c, axis=-1, keepdims=True) + jnp.sum(p_n, axis=-1, keepdims=True)
            o = (_dot_nt(p_c.astype(BF16), c_ref[0, 1, rows, :].astype(BF16))
                 + _dot_nt(p_n.astype(BF16), new[1][rows, :].astype(BF16)))
            per_head[e].append((o / l, m + jnp.log(l)))
        for kv in range(2):
            rolled = pltpu.roll(c_ref[0, kv], length - t_new, axis=1)
            if length > LANES:
                oc_refs[g][0, kv, :, 0:length - LANES] = rolled[:, 0:length - LANES]
            oc_refs[g][0, kv, :, length - LANES:length] = jnp.where(
                lane >= LANES - t_new, new[kv], rolled[:, length - LANES:length])
    for e in range(heads_per_step):
        os_, lses = zip(*per_head[e])
        o_ref[0, :, e * HEAD_DIM:(e + 1) * HEAD_DIM] = _combine_groups(list(os_), list(lses))


def _sample_mixer(q8, caches_n, kvt_new, layer, prev_outs, nseq, t_new):
    hsteps = ATT_WIDTH // LANES
    tables = [_sample_bias(g, caches_n[g].shape[-1], t_new) for g in range(N_GROUPS)]
    bc = [jnp.asarray(tc.reshape(hsteps, 2, Q_ROWS, -1)) for tc, _ in tables]
    bn = [jnp.asarray(tn.reshape(hsteps, 2, Q_ROWS, LANES)) for _, tn in tables]
    kvt4 = kvt_new.reshape(N_GROUPS, 2, ATT_WIDTH, nseq * t_new)
    assert nseq * t_new == LANES
    in_specs, args = [], []
    for g in range(N_GROUPS):
        in_specs.append(pl.BlockSpec((1, Q_ROWS, LANES), lambda hc, b, g=g: (b, 0, g * hsteps + hc)))
        args.append(q8)
    cache_specs = []
    for g in range(N_GROUPS):
        length = caches_n[g].shape[-1]
        cache_specs.append(pl.BlockSpec((1, 2, LANES, length),
                                        lambda hc, b: (layer * nseq + b, 0, hc, 0)))
    in_specs += cache_specs
    args += list(caches_n)
    for g in range(N_GROUPS):
        in_specs.append(pl.BlockSpec((1, 2, LANES, LANES), lambda hc, b, g=g: (g, 0, hc, 0)))
        args.append(kvt4)
    for tbl in bc + bn:
        in_specs.append(pl.BlockSpec((1,) + tbl.shape[1:], lambda hc, b: (hc, 0, 0, 0)))
        args.append(tbl)
    aliases = {}
    if prev_outs is not None:
        for g in range(N_GROUPS):
            in_specs.append(pl.BlockSpec(memory_space=pl.ANY))
            aliases[len(args)] = 1 + g
            args.append(prev_outs[g])
    out_shape = ([jax.ShapeDtypeStruct((nseq, Q_ROWS, ATT_WIDTH), F32)]
                 + [jax.ShapeDtypeStruct(c.shape, F32) for c in caches_n])
    out_specs = [pl.BlockSpec((1, Q_ROWS, LANES), lambda hc, b: (b, 0, hc))] + cache_specs
    outs = pl.pallas_call(
        functools.partial(_sample_kernel, t_new=t_new),
        grid=(hsteps, nseq),
        in_specs=in_specs,
        out_specs=out_specs,
        out_shape=out_shape,
        input_output_aliases=aliases,
        compiler_params=_params(2),
        name="sample_mixer",
    )(*args)
    return outs[0], outs[1:]


def _native_view(cache):
    depth, nseq, length = cache.shape[:3]
    return cache.transpose(0, 1, 3, 4, 5, 2).reshape(depth * nseq, 2, ATT_WIDTH, length)


def _from_native(x, depth, nseq):
    length = x.shape[-1]
    return x.reshape(depth, nseq, 2, N_SLOTS, HEAD_DIM, length).transpose(0, 1, 5, 2, 3, 4)


def kernel(x_prompt, x_sample, c_prompt, c_sample, cache_kv_g0, cache_kv_g1, cache_kv_g2, state_conv, w_ada, b_ada, w_in, b_in, w_dw, b_dw, conv_ln_g, conv_ln_b, w_pc, b_pc, w_pa, b_pa, w_out, b_out, ln1_g, ln1_b, w_gate, w_up, w_down, ln2_g, ln2_b):
    depth = w_in.shape[0]
    bsz, seq, d = x_prompt.shape
    nseq, t_new, _ = x_sample.shape
    alpha = (2 * depth) ** 0.25
    tm = 512
    m_s = nseq * t_new

    mod_all = _adaln(jnp.concatenate([c_prompt, c_sample], axis=0), w_ada, b_ada)
    caches_n = [_native_view(c) for c in (cache_kv_g0, cache_kv_g1, cache_kv_g2)]
    state_n = state_conv.transpose(0, 2, 1, 3)

    vec = lambda v: v.reshape(depth, 1, -1)
    w_in_bf = w_in.astype(BF16)
    b_in3 = vec(b_in)
    conv_w = [w_dw, vec(b_dw), vec(conv_ln_g), vec(conv_ln_b)]
    merge_w = [w_pc.astype(BF16), vec(b_pc), w_pa.astype(BF16), vec(b_pa),
               w_out.astype(BF16), vec(b_out), vec(ln1_g), vec(ln1_b)]
    ffn_w = [w_gate.astype(BF16), w_up.astype(BF16), w_down.astype(BF16), vec(ln2_g), vec(ln2_b)]

    xp = x_prompt
    xs = x_sample.transpose(1, 0, 2).reshape(1, m_s, d)
    conv_prompt, conv_sample = [], []
    cache_outs = None
    kv_nat = None
    for l in range(depth):
        kv_cols = []
        for g in range(N_GROUPS):
            kv_cols += [slice(_k_col(g), _k_col(g) + ATT_WIDTH), slice(_v_col(g), _v_col(g) + ATT_WIDTH)]
        w_bt = jnp.concatenate([w_in_bf[l][:, c] for c in kv_cols], axis=1).T
        b_bt = jnp.concatenate([b_in[l][c] for c in kv_cols]).reshape(-1, 1)
        mod_p = mod_all[l, :bsz].reshape(bsz, 1, MOD_WIDTH)
        mod_s = jnp.tile(mod_all[l, bsz:], (t_new, 1)).reshape(1, m_s, MOD_WIDTH)
        mod_sq = jnp.repeat(mod_all[l, bsz:], t_new, axis=0).reshape(1, m_s, MOD_WIDTH)

        glu, gate_c, gate_a, *qx = _inproj_a(xp, mod_p, w_in_bf, b_in3, l, tm)
        kvb, kv_nat, ycv = _inproj_b(xp, mod_p, w_in_bf, b_in3, w_bt, b_bt, glu, conv_w, l, depth,
                                     kv_nat, tm)
        conv_prompt.append(glu[:, seq - (CONV_WIDTH - 1):, :])
        os_, lses = [], []
        for g in range(N_GROUPS):
            o, lse = _prompt_attention(qx[g], kvb[g], DILATIONS[g])
            os_.append(o)
            lses.append(lse)
        x1 = _post_a(xp, mod_p, ycv, os_, lses, gate_c, gate_a, merge_w, l, tm, alpha)
        xp = _post_b(x1, mod_p, ffn_w, l, tm, alpha)

        xs_q = xs.reshape(t_new, nseq, d).transpose(1, 0, 2).reshape(1, m_s, d)
        glu_s, gc_s, ga_s, q_s, kvt_new = _inproj_s(xs, mod_s, xs_q, mod_sq, w_in_bf, b_in3, w_bt, b_bt, l)
        q8 = jnp.pad(q_s.reshape(t_new, nseq, N_GROUPS * ATT_WIDTH).transpose(1, 0, 2),
                     ((0, 0), (0, Q_ROWS - t_new), (0, 0)))
        o8, cache_outs = _sample_mixer(q8, caches_n, kvt_new, l, cache_outs, nseq, t_new)
        o_s = o8[:, :t_new].transpose(1, 0, 2).reshape(1, m_s, ATT_WIDTH)
        xs, nconv = _post_s(xs, mod_s, state_n, l, glu_s, o_s, gc_s, ga_s, conv_w + merge_w + ffn_w,
                            nseq, t_new, alpha)
        conv_sample.append(nconv[0])

    kv_p = [_from_native(kv_nat[g].reshape(depth * bsz, 2, ATT_WIDTH, -1), depth, bsz)
            for g in range(N_GROUPS)]
    kv_s = [_from_native(cache_outs[g], depth, nseq) for g in range(N_GROUPS)]
    return (xp, xs.reshape(t_new, nseq, d).transpose(1, 0, 2),
            kv_p[0], kv_p[1], kv_p[2], jnp.stack(conv_prompt),
            kv_s[0], kv_s[1], kv_s[2], jnp.stack(conv_sample).transpose(0, 2, 1, 3))
```

```python
import functools

import numpy as np
import jax
import jax.numpy as jnp
from jax import lax
from jax.experimental import pallas as pl
from jax.experimental.pallas import tpu as pltpu

D_MODEL = 1024
CONV_CH = 512
CONV_WIDTH = 31
N_SLOTS = 8
HEAD_DIM = 64
ATT_WIDTH = N_SLOTS * HEAD_DIM
WINDOWS = (128, 512, 2048)
DILATIONS = (1, 4, 16)
N_GROUPS = 3
SPAN = 128
BLOCK = 128
LN_EPS = 1e-5
QKV_OFF = 2 * CONV_CH
GATE_OFF = QKV_OFF + 3 * N_GROUPS * ATT_WIDTH
IN_WIDTH = GATE_OFF + 2 * D_MODEL
KV_WIDTH = 2 * ATT_WIDTH
MOD_WIDTH = 6 * D_MODEL
LANES = 128
NEG = -1e30
Q_ROWS = 8
CONV_HALO = 32
VMEM_LIMIT = 56 * 1024 * 1024

F32 = jnp.float32
BF16 = jnp.bfloat16
GATE_DTYPE = jnp.bfloat16

_NT = (((1,), (1,)), ((), ()))


def _q_col(g):
    return QKV_OFF + g * ATT_WIDTH


def _k_col(g):
    return QKV_OFF + (N_GROUPS + g) * ATT_WIDTH


def _v_col(g):
    return QKV_OFF + (2 * N_GROUPS + g) * ATT_WIDTH


def _dot(a, b):
    return jnp.dot(a, b, preferred_element_type=F32)


def _dot_nt(a, b):
    return lax.dot_general(a, b, _NT, preferred_element_type=F32)


def _sigmoid(x):
    return 1.0 / (1.0 + jnp.exp(-x))


def _silu(x):
    return x * _sigmoid(x)


def _layer_norm(x, g, b):
    mu = jnp.mean(x, axis=-1, keepdims=True)
    xc = x - mu
    var = jnp.mean(xc * xc, axis=-1, keepdims=True)
    return xc * lax.rsqrt(var + LN_EPS) * g + b


def _mod_slice(m, i):
    return m[:, i * D_MODEL:(i + 1) * D_MODEL]


def _const_spec(shape):
    nd = len(shape)
    return pl.BlockSpec(shape, lambda *_: (0,) * nd, pipeline_mode=pl.Buffered(1))


def _layer_spec(arr, layer):
    nd = arr.ndim
    return pl.BlockSpec((1,) + arr.shape[1:], lambda *_: (layer,) + (0,) * (nd - 1),
                        pipeline_mode=pl.Buffered(1))


def _params(n_axes):
    return pltpu.CompilerParams(dimension_semantics=("arbitrary",) * n_axes,
                                vmem_limit_bytes=VMEM_LIMIT)


def _adaln_kernel(c_ref, w_ref, b_ref, o_ref):
    c = c_ref[...]
    s = _silu(c).astype(BF16)
    o_ref[0] = _dot(s, w_ref[0].astype(BF16)) + b_ref[0]


def _adaln(c_all, w_ada, b_ada):
    depth, d, width = w_ada.shape
    rows = c_all.shape[0]
    tn = 1024
    return pl.pallas_call(
        _adaln_kernel,
        grid=(depth, width // tn),
        in_specs=[pl.BlockSpec((rows, d), lambda l, j: (0, 0)),
                  pl.BlockSpec((1, d, tn), lambda l, j: (l, 0, j)),
                  pl.BlockSpec((1, 1, tn), lambda l, j: (l, 0, j))],
        out_specs=pl.BlockSpec((1, rows, tn), lambda l, j: (l, 0, j)),
        out_shape=jax.ShapeDtypeStruct((depth, rows, width), F32),
        compiler_params=_params(2),
        name="adaln",
    )(c_all, w_ada, b_ada.reshape(depth, 1, width))


def _modulated(x_ref, mod_ref, shift_i, scale_i):
    m = mod_ref[0]
    x = x_ref[0]
    return (x * (1.0 + _mod_slice(m, scale_i)) + _mod_slice(m, shift_i)).astype(BF16)


def _proj(u, w_ref, b_ref, c0, width):
    return _dot(u, w_ref[0, :, c0:c0 + width]) + b_ref[0, :, c0:c0 + width]


def _glu_gates_q(u, w_ref, b_ref):
    glu = _proj(u, w_ref, b_ref, 0, CONV_CH) * _sigmoid(_proj(u, w_ref, b_ref, CONV_CH, CONV_CH))
    gate_c = _sigmoid(_proj(u, w_ref, b_ref, GATE_OFF, D_MODEL))
    gate_a = _sigmoid(_proj(u, w_ref, b_ref, GATE_OFF + D_MODEL, D_MODEL))
    qs = [_proj(u, w_ref, b_ref, _q_col(g), ATT_WIDTH) * (HEAD_DIM ** -0.5) for g in range(N_GROUPS)]
    return glu, gate_c, gate_a, qs


def _residue_rows(t, scr_ref, slot, dil):
    if dil == 1:
        return [t]
    rows = t.shape[0]
    scr_ref[slot] = t
    return [scr_ref[slot, pl.ds(r, rows // dil, stride=dil), :] for r in range(dil)]


def _inproj_a_kernel(x_ref, mod_ref, w_ref, b_ref, glu_ref, gc_ref, ga_ref, q0_ref, q1_ref, q2_ref,
                     scr_ref):
    u = _modulated(x_ref, mod_ref, 0, 1)
    glu, gate_c, gate_a, qs = _glu_gates_q(u, w_ref, b_ref)
    glu_ref[0] = glu
    gc_ref[0] = gate_c.astype(gc_ref.dtype)
    ga_ref[0] = gate_a.astype(ga_ref.dtype)
    for g, (q, q_ref) in enumerate(zip(qs, (q0_ref, q1_ref, q2_ref))):
        dil = DILATIONS[g]
        for p in range(ATT_WIDTH // LANES):
            parts = _residue_rows(q[:, p * LANES:(p + 1) * LANES], scr_ref, p, dil)
            for r, t in enumerate(parts):
                lane = lax.broadcasted_iota(jnp.int32, t.shape, 1)
                q_ref[0, r, :, (2 * p) * LANES:(2 * p + 1) * LANES] = jnp.where(lane < HEAD_DIM, t, 0.0).astype(BF16)
                q_ref[0, r, :, (2 * p + 1) * LANES:(2 * p + 2) * LANES] = jnp.where(lane >= HEAD_DIM, t, 0.0).astype(BF16)


def _residue_spec(tm, dil, width):
    return pl.BlockSpec((1, dil, tm // dil, width), lambda b, i: (b, 0, i, 0))


def _inproj_a(x, mod, w_in, b_in, layer, tm):
    bsz, s, d = x.shape
    nt = s // tm
    tile = lambda width: pl.BlockSpec((1, tm, width), lambda b, i: (b, i, 0))
    out_shape = ([jax.ShapeDtypeStruct((bsz, s, CONV_CH), F32),
                  jax.ShapeDtypeStruct((bsz, s, d), GATE_DTYPE),
                  jax.ShapeDtypeStruct((bsz, s, d), GATE_DTYPE)]
                 + [jax.ShapeDtypeStruct((bsz, dil, s // dil, 2 * ATT_WIDTH), BF16) for dil in DILATIONS])
    return pl.pallas_call(
        _inproj_a_kernel,
        grid=(bsz, nt),
        in_specs=[tile(d),
                  pl.BlockSpec((1, 1, MOD_WIDTH), lambda b, i: (b, 0, 0)),
                  _layer_spec(w_in, layer), _layer_spec(b_in, layer)],
        out_specs=[tile(CONV_CH), tile(d), tile(d)]
                  + [_residue_spec(tm, dil, 2 * ATT_WIDTH) for dil in DILATIONS],
        out_shape=out_shape,
        scratch_shapes=[pltpu.VMEM((ATT_WIDTH // LANES, tm, LANES), F32)],
        compiler_params=_params(2),
        name="inproj_a",
    )(x, mod, w_in, b_in)


def _conv_ln_silu(win_ref, rows, wdw_ref, bdw_ref, cg_ref, cb_ref, out_ref, rb):
    off = CONV_HALO - (CONV_WIDTH - 1)
    nslab = CONV_CH // LANES
    for r0 in range(0, rows, rb):
        accs = []
        for c in range(nslab):
            cols = slice(c * LANES, (c + 1) * LANES)
            acc = jnp.zeros((rb, LANES), F32) + bdw_ref[0, :, cols]
            for j in range(CONV_WIDTH):
                acc = acc + win_ref[c, pl.ds(r0 + off + j, rb, stride=1), :] * wdw_ref[0, j:j + 1, cols]
            accs.append(acc)
        y = _silu(_layer_norm(jnp.concatenate(accs, axis=1), cg_ref[0], cb_ref[0]))
        out_ref[0, r0:r0 + rb, :] = y.astype(out_ref.dtype)


def _inproj_b_kernel(*refs, tm, seq, aliased):
    (x_ref, mod_ref, w_ref, b_ref, glu_ref, halo_ref,
     wdw_ref, bdw_ref, cg_ref, cb_ref) = refs[:10]
    n_in = 10 + (N_GROUPS if aliased else 0)
    kv_refs = refs[n_in:n_in + N_GROUPS]
    nat_refs = refs[n_in + N_GROUPS:n_in + 2 * N_GROUPS]
    ycv_ref, scr_ref, win_ref, keep_ref = refs[n_in + 2 * N_GROUPS:]
    ti = pl.program_id(1)
    halo = halo_ref[0]
    halo = jnp.where(ti > 0, halo, jnp.zeros_like(halo))
    glu = glu_ref[0]
    for c in range(CONV_CH // LANES):
        win_ref[c, 0:CONV_HALO, :] = halo[:, c * LANES:(c + 1) * LANES]
        win_ref[c, CONV_HALO:CONV_HALO + tm, :] = glu[:, c * LANES:(c + 1) * LANES]
    _conv_ln_silu(win_ref, tm, wdw_ref, bdw_ref, cg_ref, cb_ref, ycv_ref, rb=64)
    u = _modulated(x_ref, mod_ref, 0, 1)
    wins = [min(w, seq) for w in WINDOWS]
    every_tile = [w >= seq for w in wins]
    for g, kv_ref in enumerate(kv_refs):
        for half, c0 in enumerate((_k_col(g), _v_col(g))):
            t = _proj(u, w_ref, b_ref, c0, ATT_WIDTH)
            for p in range(ATT_WIDTH // LANES):
                parts = _residue_rows(t[:, p * LANES:(p + 1) * LANES], scr_ref, half * 4 + p, DILATIONS[g])
                for r, part in enumerate(parts):
                    cols = slice(half * ATT_WIDTH + p * LANES, half * ATT_WIDTH + (p + 1) * LANES)
                    kv_ref[0, r, :, cols] = part.astype(BF16)
            if every_tile[g]:
                nat_refs[g][0, half * ATT_WIDTH:(half + 1) * ATT_WIDTH, :] = t.T
            else:
                keep_ref[2 * g + half] = t
    for g, n_ref in enumerate(nat_refs):
        if every_tile[g]:
            continue
        win = wins[g]
        rows = min(win, tm)
        cond = (ti >= (seq - win) // tm) if win >= tm else (ti == seq // tm - 1)

        @pl.when(cond)
        def _(g=g, n_ref=n_ref, rows=rows):
            for half in range(2):
                n_ref[0, half * ATT_WIDTH:(half + 1) * ATT_WIDTH, :] = keep_ref[2 * g + half, tm - rows:tm, :].T


def _inproj_b(x, mod, w_in, b_in, glu, conv_w, layer, depth, prev_nat, tm):
    bsz, s, d = x.shape
    nt = s // tm
    assert tm % CONV_HALO == 0
    tile = lambda width: pl.BlockSpec((1, tm, width), lambda b, i: (b, i, 0))
    halo_spec = pl.BlockSpec(
        (1, CONV_HALO, CONV_CH), lambda b, i: (b, jnp.maximum(i * (tm // CONV_HALO) - 1, 0), 0))
    nat_specs, nat_shapes = [], []
    for g in range(N_GROUPS):
        win = min(WINDOWS[g], s)
        assert win % tm == 0 or tm % win == 0
        nat_shapes.append(jax.ShapeDtypeStruct((depth * bsz, KV_WIDTH, win), F32))
        if win >= tm:
            first = (s - win) // tm
            nat_specs.append(pl.BlockSpec(
                (1, KV_WIDTH, tm),
                lambda b, i, first=first: (layer * bsz + b, 0, jnp.maximum(i - first, 0))))
        else:
            nat_specs.append(pl.BlockSpec((1, KV_WIDTH, win), lambda b, i: (layer * bsz + b, 0, 0)))
    args = [x, mod, w_in, b_in, glu, glu, *conv_w]
    in_specs = [tile(d), pl.BlockSpec((1, 1, MOD_WIDTH), lambda b, i: (b, 0, 0)),
                _layer_spec(w_in, layer), _layer_spec(b_in, layer),
                tile(CONV_CH), halo_spec] + [_layer_spec(w, layer) for w in conv_w]
    aliases = {}
    if prev_nat is not None:
        for g in range(N_GROUPS):
            in_specs.append(pl.BlockSpec(memory_space=pl.ANY))
            aliases[len(args)] = N_GROUPS + g
            args.append(prev_nat[g])
    outs = pl.pallas_call(
        functools.partial(_inproj_b_kernel, tm=tm, seq=s, aliased=prev_nat is not None),
        grid=(bsz, nt),
        in_specs=in_specs,
        out_specs=[_residue_spec(tm, dil, KV_WIDTH) for dil in DILATIONS] + nat_specs + [tile(CONV_CH)],
        out_shape=[jax.ShapeDtypeStruct((bsz, dil, s // dil, KV_WIDTH), BF16) for dil in DILATIONS]
                  + nat_shapes + [jax.ShapeDtypeStruct((bsz, s, CONV_CH), BF16)],
        scratch_shapes=[pltpu.VMEM((KV_WIDTH // LANES, tm, LANES), F32),
                        pltpu.VMEM((CONV_CH // LANES, CONV_HALO + tm, LANES), F32),
                        pltpu.VMEM((2 * N_GROUPS, tm, ATT_WIDTH), F32)],
        input_output_aliases=aliases,
        compiler_params=_params(2),
        name="inproj_b",
    )(*args)
    return outs[:N_GROUPS], outs[N_GROUPS:2 * N_GROUPS], outs[2 * N_GROUPS]


def _inproj_s_kernel(x_ref, mod_ref, xq_ref, modq_ref, w_ref, b_ref,
                     glu_ref, gc_ref, ga_ref, q_ref, kvt_ref):
    u = _modulated(x_ref, mod_ref, 0, 1)
    glu, gate_c, gate_a, qs = _glu_gates_q(u, w_ref, b_ref)
    glu_ref[0] = glu
    gc_ref[0] = gate_c.astype(gc_ref.dtype)
    ga_ref[0] = gate_a.astype(ga_ref.dtype)
    for g, q in enumerate(qs):
        q_ref[0, :, g * ATT_WIDTH:(g + 1) * ATT_WIDTH] = q
    uq = _modulated(xq_ref, modq_ref, 0, 1)
    for g in range(N_GROUPS):
        for half, c0 in enumerate((_k_col(g), _v_col(g))):
            r0 = g * KV_WIDTH + half * ATT_WIDTH
            kvt_ref[r0:r0 + ATT_WIDTH, :] = _proj(uq, w_ref, b_ref, c0, ATT_WIDTH).T


def _inproj_s(x, mod, xq, modq, w_in, b_in, layer):
    _, m, d = x.shape
    full = lambda *shape: _const_spec(shape)
    whole = lambda *shape: pl.BlockSpec(shape, lambda i: (0,) * len(shape))
    return pl.pallas_call(
        _inproj_s_kernel,
        in_specs=[full(1, m, d), full(1, m, MOD_WIDTH), full(1, m, d), full(1, m, MOD_WIDTH),
                  _layer_spec(w_in, layer), _layer_spec(b_in, layer)],
        out_specs=[whole(1, m, CONV_CH), whole(1, m, d), whole(1, m, d),
                   whole(1, m, N_GROUPS * ATT_WIDTH), whole(N_GROUPS * KV_WIDTH, m)],
        out_shape=[jax.ShapeDtypeStruct((1, m, CONV_CH), F32),
                   jax.ShapeDtypeStruct((1, m, d), GATE_DTYPE),
                   jax.ShapeDtypeStruct((1, m, d), GATE_DTYPE),
                   jax.ShapeDtypeStruct((1, m, N_GROUPS * ATT_WIDTH), F32),
                   jax.ShapeDtypeStruct((N_GROUPS * KV_WIDTH, m), F32)],
        grid=(1,),
        compiler_params=_params(1),
        name="inproj_s",
    )(x, mod, xq, modq, w_in, b_in)


def _alibi_slopes():
    return (2.0 ** (-8.0 * (np.arange(N_SLOTS) + 1) / N_SLOTS)).astype(np.float32)


def _prompt_bias(dil, key_blocks):
    qi = np.arange(BLOCK)[:, None]
    kj = np.arange(key_blocks * BLOCK)[None, :]
    tables = []
    for lead in (0, key_blocks - 1):
        rel = lead * BLOCK + qi - kj
        valid = (rel >= 0) & (rel <= SPAN)
        bias = -_alibi_slopes()[:, None, None] * (rel * dil).astype(np.float32)[None]
        tables.append(np.where(valid[None], bias, np.float32(NEG)))
    return np.stack(tables).astype(np.float32)


def _attn_kernel(q_ref, kv_ref, bias_ref, o_ref, lse_ref, s_scr, m_scr, *, nb, n_units):
    kw = bias_ref.shape[-1]
    lane = lax.broadcasted_iota(jnp.int32, (BLOCK, LANES), 1)
    pick = lambda a, b: jnp.where(lane < HEAD_DIM, a, b)

    def place(u):
        if isinstance(u, int):
            rr, c = divmod(u, nb)
            r0 = c * BLOCK
            k0 = max(r0 - (kw - BLOCK), 0)
            return rr, r0, k0, (0 if c == 0 else 1)
        rr = u // nb
        c = u % nb
        r0 = pl.multiple_of(c * BLOCK, BLOCK)
        k0 = pl.multiple_of(jnp.maximum(r0 - (kw - BLOCK), 0), BLOCK)
        return rr, r0, k0, jnp.minimum(c, 1)

    ones = jnp.ones((kw, LANES), BF16)

    def scores(u, slot):
        rr, r0, k0, tbl = place(u)
        for p in range(ATT_WIDTH // LANES):
            kp = kv_ref[0, rr, pl.ds(k0, kw), p * LANES:(p + 1) * LANES]
            q2 = q_ref[0, rr, pl.ds(r0, BLOCK), 2 * p * LANES:(2 * p + 2) * LANES]
            q2 = jnp.concatenate([q2[:, :LANES], q2[:, LANES:]], axis=0)
            bias2 = jnp.concatenate([bias_ref[tbl, 2 * p], bias_ref[tbl, 2 * p + 1]], axis=0)
            s = _dot_nt(q2, kp) + bias2
            s_scr[slot, p] = s
            m_scr[slot, p] = jnp.max(s, axis=-1, keepdims=True)

    def finish(u, slot):
        rr, r0, k0, _ = place(u)
        for p in range(ATT_WIDTH // LANES):
            vp = kv_ref[0, rr, pl.ds(k0, kw), ATT_WIDTH + p * LANES:ATT_WIDTH + (p + 1) * LANES]
            m = m_scr[slot, p]
            pexp = jnp.exp(s_scr[slot, p] - m).astype(BF16)
            pv = _dot(pexp, jnp.concatenate([vp, ones], axis=1))
            l2 = pick(pv[:BLOCK, LANES:], pv[BLOCK:, LANES:])
            cols = slice(p * LANES, (p + 1) * LANES)
            o_ref[0, rr, pl.ds(r0, BLOCK), cols] = pick(pv[:BLOCK, :LANES], pv[BLOCK:, :LANES]) / l2
            lse_ref[0, rr, pl.ds(r0, BLOCK), cols] = pick(m[:BLOCK], m[BLOCK:]) + jnp.log(l2)

    scores(0, 0)

    def body(u, carry):
        slot = u % 2
        finish(u, slot)
        scores(u + 1, 1 - slot)
        return carry
    lax.fori_loop(0, n_units - 1, body, 0)
    finish(n_units - 1, (n_units - 1) % 2)


def _prompt_attention(qx, kvb, dil):
    bsz, _, n, _ = qx.shape
    assert n % BLOCK == 0
    nb = n // BLOCK
    nres = max(1, min(dil, 16 // nb))
    assert dil % nres == 0
    bias = jnp.asarray(_prompt_bias(dil, min(nb, 2)))
    sub = lambda width: pl.BlockSpec((1, nres, n, width), lambda b, r: (b, r, 0, 0))
    return pl.pallas_call(
        functools.partial(_attn_kernel, nb=nb, n_units=nres * nb),
        grid=(bsz, dil // nres),
        in_specs=[sub(2 * ATT_WIDTH), sub(KV_WIDTH), _const_spec(bias.shape)],
        out_specs=[sub(ATT_WIDTH), sub(ATT_WIDTH)],
        out_shape=[jax.ShapeDtypeStruct((bsz, dil, n, ATT_WIDTH), F32)] * 2,
        scratch_shapes=[pltpu.VMEM((2, N_SLOTS // 2, 2 * BLOCK, bias.shape[-1]), F32),
                        pltpu.VMEM((2, N_SLOTS // 2, 2 * BLOCK, 1), F32)],
        compiler_params=_params(2),
        name=f"attn_d{dil}",
    )(qx, kvb, bias)


def _combine_groups(os_, lses):
    m = functools.reduce(jnp.maximum, lses)
    ws = [jnp.exp(l - m) for l in lses]
    den = functools.reduce(lambda a, b: a + b, ws)
    num = functools.reduce(lambda a, b: a + b, [w * o for w, o in zip(ws, os_)])
    return num / den


def _merge_to_ln1(x, m, ybuf, o_att, gate_c, gate_a,
                  wpc_ref, bpc_ref, wpa_ref, bpa_ref, wout_ref, bout_ref, g1_ref, b1_ref, alpha):
    y_conv = _dot(ybuf, wpc_ref[0]) + bpc_ref[0]
    y_att = _dot(o_att.astype(BF16), wpa_ref[0]) + bpa_ref[0]
    y = (gate_c * y_conv + gate_a * y_att).astype(BF16)
    y = _dot(y, wout_ref[0]) + bout_ref[0]
    return _layer_norm(alpha * x + (1.0 + _mod_slice(m, 2)) * y, g1_ref[0], b1_ref[0])


def _ffn_to_ln2(x1, m, wg_ref, wu_ref, wd_ref, g2_ref, b2_ref, alpha, chunk):
    u2 = (x1 * (1.0 + _mod_slice(m, 4)) + _mod_slice(m, 3)).astype(BF16)
    d_ff = wg_ref.shape[-1]
    h = None
    for c0 in range(0, d_ff, chunk):
        c1 = min(c0 + chunk, d_ff)
        a = (_silu(_dot(u2, wg_ref[0, :, c0:c1])) * _dot(u2, wu_ref[0, :, c0:c1])).astype(BF16)
        part = _dot(a, wd_ref[0, c0:c1, :])
        h = part if h is None else h + part
    return _layer_norm(alpha * x1 + (1.0 + _mod_slice(m, 5)) * h, g2_ref[0], b2_ref[0])


def _token_order(ref, scr_ref, slot, p, dil):
    cols = slice(p * LANES, (p + 1) * LANES)
    if dil == 1:
        return ref[0, 0, :, cols]
    n = ref.shape[2]
    for r in range(dil):
        scr_ref[slot, pl.ds(r, n, stride=dil), :] = ref[0, r, :, cols]
    return scr_ref[slot]


def _post_a_kernel(x_ref, mod_ref, ycv_ref, o0_ref, o1_ref, o2_ref, l0_ref, l1_ref, l2_ref,
                   gc_ref, ga_ref, wpc_ref, bpc_ref, wpa_ref, bpa_ref, wout_ref, bout_ref,
                   g1_ref, b1_ref, out_ref, scr_ref, *, alpha):
    o_refs = (o0_ref, o1_ref, o2_ref)
    l_refs = (l0_ref, l1_ref, l2_ref)
    chunks = []
    for p in range(ATT_WIDTH // LANES):
        os_ = [_token_order(o_refs[g], scr_ref, 2 * g, p, DILATIONS[g]) for g in range(N_GROUPS)]
        lses = [_token_order(l_refs[g], scr_ref, 2 * g + 1, p, DILATIONS[g]) for g in range(N_GROUPS)]
        chunks.append(_combine_groups(os_, lses).astype(BF16))
    o_att = jnp.concatenate(chunks, axis=1)
    out_ref[0] = _merge_to_ln1(x_ref[0], mod_ref[0], ycv_ref[0], o_att,
                               gc_ref[0].astype(F32), ga_ref[0].astype(F32),
                               wpc_ref, bpc_ref, wpa_ref, bpa_ref, wout_ref, bout_ref,
                               g1_ref, b1_ref, alpha)


def _post_a(x, mod, ycv, os_, lses, gate_c, gate_a, wts, layer, tm, alpha):
    bsz, s, d = x.shape
    nt = s // tm
    tile = lambda width: pl.BlockSpec((1, tm, width), lambda b, i: (b, i, 0))
    res_specs = [_residue_spec(tm, dil, ATT_WIDTH) for dil in DILATIONS]
    return pl.pallas_call(
        functools.partial(_post_a_kernel, alpha=alpha),
        grid=(bsz, nt),
        in_specs=([tile(d), pl.BlockSpec((1, 1, MOD_WIDTH), lambda b, i: (b, 0, 0)), tile(CONV_CH)]
                  + res_specs + res_specs + [tile(d), tile(d)]
                  + [_layer_spec(w, layer) for w in wts]),
        out_specs=tile(d),
        out_shape=jax.ShapeDtypeStruct((bsz, s, d), F32),
        scratch_shapes=[pltpu.VMEM((2 * N_GROUPS, tm, LANES), F32)],
        compiler_params=_params(2),
        name="post_a",
    )(x, mod, ycv, *os_, *lses, gate_c, gate_a, *wts)


def _post_b_kernel(*refs, alpha, chunk, t_new, nsq, nblk):
    x_ref, mod_ref, wg_ref, wu_ref, wd_ref, g2_ref, b2_ref = refs[:7]
    q_refs, c_refs, n_refs = refs[7:10], refs[10:13], refs[13:16]
    bc_refs, bn_refs = refs[16:19], refs[19:22]
    out_ref, o_ref = refs[-5:-3]
    oc_refs = refs[-3:]
    out_ref[0] = _ffn_to_ln2(x_ref[0], mod_ref[0], wg_ref, wu_ref, wd_ref, g2_ref, b2_ref, alpha, chunk)
    step = pl.program_id(0) * pl.num_programs(1) + pl.program_id(1)
    first_seq = (step % nblk) * nsq
    for i in range(nsq):
        _sample_unit(q_refs, c_refs, n_refs, bc_refs, bn_refs, o_ref, oc_refs, i, first_seq + i, t_new)


def _post_b(x1, mod, wts, layer, tm, alpha, q8, caches_n, kvt_new, prev_outs, nseq, t_new):
    bsz, s, d = x1.shape
    nt = s // tm
    hsteps = ATT_WIDTH // LANES
    steps = bsz * nt
    assert (hsteps * nseq) % steps == 0 and nseq * t_new == LANES
    nsq = hsteps * nseq // steps
    nblk = nseq // nsq
    side = lambda f: (lambda b, i: f(*divmod(b * nt + i, nblk)))
    tables = [_sample_bias(g, caches_n[g].shape[-1], t_new) for g in range(N_GROUPS)]
    bc = [jnp.asarray(tc.reshape(hsteps, 2, Q_ROWS, -1)) for tc, _ in tables]
    bn = [jnp.asarray(tn.reshape(hsteps, 2, Q_ROWS, LANES)) for _, tn in tables]
    kvt4 = kvt_new.reshape(N_GROUPS, 2, ATT_WIDTH, nseq * t_new)
    tile = pl.BlockSpec((1, tm, d), lambda b, i: (b, i, 0))
    args = [x1, mod, *wts]
    in_specs = ([tile, pl.BlockSpec((1, 1, MOD_WIDTH), lambda b, i: (b, 0, 0))]
                + [_layer_spec(w, layer) for w in wts])
    for g in range(N_GROUPS):
        in_specs.append(pl.BlockSpec((nsq, Q_ROWS, LANES), side(lambda hc, blk, g=g: (blk, 0, g * hsteps + hc))))
        args.append(q8)
    cache_specs = [pl.BlockSpec((nsq, 2, LANES, c.shape[-1]),
                                side(lambda hc, blk: (layer * nblk + blk, 0, hc, 0))) for c in caches_n]
    in_specs += cache_specs
    args += list(caches_n)
    for g in range(N_GROUPS):
        in_specs.append(pl.BlockSpec((1, 2, LANES, LANES), side(lambda hc, blk, g=g: (g, 0, hc, 0))))
        args.append(kvt4)
    for tbl in bc + bn:
        in_specs.append(pl.BlockSpec((1,) + tbl.shape[1:], side(lambda hc, blk: (hc, 0, 0, 0))))
        args.append(tbl)
    aliases = {}
    if prev_outs is not None:
        for g in range(N_GROUPS):
            in_specs.append(pl.BlockSpec(memory_space=pl.ANY))
            aliases[len(args)] = 2 + g
            args.append(prev_outs[g])
    outs = pl.pallas_call(
        functools.partial(_post_b_kernel, alpha=alpha, chunk=1024, t_new=t_new, nsq=nsq, nblk=nblk),
        grid=(bsz, nt),
        in_specs=in_specs,
        out_specs=[tile, pl.BlockSpec((nsq, Q_ROWS, LANES), side(lambda hc, blk: (blk, 0, hc)))] + cache_specs,
        out_shape=[jax.ShapeDtypeStruct((bsz, s, d), F32),
                   jax.ShapeDtypeStruct((nseq, Q_ROWS, ATT_WIDTH), F32)]
                  + [jax.ShapeDtypeStruct(c.shape, F32) for c in caches_n],
        input_output_aliases=aliases,
        compiler_params=_params(2),
        name="post_b",
    )(*args)
    return outs[0], outs[1], outs[2:]


def _post_s_kernel(x_ref, mod_ref, state_ref, glu_ref, o_ref, gc_ref, ga_ref,
                   wdw_ref, bdw_ref, cg_ref, cb_ref,
                   wpc_ref, bpc_ref, wpa_ref, bpa_ref, wout_ref, bout_ref, g1_ref, b1_ref,
                   wg_ref, wu_ref, wd_ref, g2_ref, b2_ref,
                   out_ref, nconv_ref, ypre_ref, *, nseq, t_new, alpha, chunk):
    ctx = CONV_WIDTH - 1

    def slab(i):
        if i < ctx:
            return state_ref[0, i]
        return glu_ref[0, (i - ctx) * nseq:(i - ctx + 1) * nseq, :]

    for t in range(t_new):
        acc = jnp.zeros((nseq, CONV_CH), F32) + bdw_ref[0]
        for j in range(CONV_WIDTH):
            acc = acc + slab(t + j) * wdw_ref[0, j:j + 1, :]
        ypre_ref[t * nseq:(t + 1) * nseq, :] = acc
    for i in range(ctx):
        nconv_ref[0, i] = slab(i + t_new)
    ybuf = _silu(_layer_norm(ypre_ref[...], cg_ref[0], cb_ref[0])).astype(BF16)
    m = mod_ref[0]
    x1 = _merge_to_ln1(x_ref[0], m, ybuf, o_ref[0], gc_ref[0].astype(F32), ga_ref[0].astype(F32),
                       wpc_ref, bpc_ref, wpa_ref, bpa_ref, wout_ref, bout_ref, g1_ref, b1_ref, alpha)
    out_ref[0] = _ffn_to_ln2(x1, m, wg_ref, wu_ref, wd_ref, g2_ref, b2_ref, alpha, chunk)


def _post_s(x, mod, state_n, layer, glu, o_att, gate_c, gate_a, wts, nseq, t_new, alpha):
    _, m, d = x.shape
    ctx = CONV_WIDTH - 1
    full = lambda *shape: _const_spec(shape)
    whole = lambda *shape: pl.BlockSpec(shape, lambda i: (0,) * len(shape))
    return pl.pallas_call(
        functools.partial(_post_s_kernel, nseq=nseq, t_new=t_new, alpha=alpha, chunk=1024),
        in_specs=[full(1, m, d), full(1, m, MOD_WIDTH), _layer_spec(state_n, layer), full(1, m, CONV_CH),
                  full(1, m, ATT_WIDTH), full(1, m, d), full(1, m, d)]
                 + [_layer_spec(w, layer) for w in wts],
        out_specs=[whole(1, m, d), whole(1, ctx, nseq, CONV_CH)],
        out_shape=[jax.ShapeDtypeStruct((1, m, d), F32),
                   jax.ShapeDtypeStruct((1, ctx, nseq, CONV_CH), F32)],
        scratch_shapes=[pltpu.VMEM((m, CONV_CH), F32)],
        grid=(1,),
        compiler_params=_params(1),
        name="post_s",
    )(x, mod, state_n, glu, o_att, gate_c, gate_a, *wts)


def _sample_bias(g, length, t_new):
    dil = DILATIONS[g]
    slopes = _alibi_slopes()[:, None, None]
    t = np.arange(Q_ROWS)[:, None]
    p = np.arange(length)[None, :]
    dist = length + t - p
    valid = (dist % dil == 0) & (dist // dil >= 1) & (dist // dil <= SPAN) & (t < t_new)
    cache = np.where(valid[None], -slopes * dist.astype(np.float32)[None], np.float32(NEG))
    tp = np.arange(LANES)[None, :] - (LANES - t_new)
    dist_n = t - tp
    valid_n = (tp >= 0) & (dist_n >= 0) & (dist_n % dil == 0) & (dist_n // dil <= SPAN) & (t < t_new)
    new = np.where(valid_n[None], -slopes * dist_n.astype(np.float32)[None], np.float32(NEG))
    return cache.astype(np.float32), new.astype(np.float32)


def _sample_unit(q_refs, c_refs, n_refs, bc_refs, bn_refs, o_ref, oc_refs, i, seq, t_new):
    shift = (LANES - t_new) - t_new * seq
    lane = lax.broadcasted_iota(jnp.int32, (LANES, LANES), 1)
    heads_per_step = LANES // HEAD_DIM
    per_head = [[] for _ in range(heads_per_step)]
    for g in range(N_GROUPS):
        c_ref = c_refs[g]
        length = c_ref.shape[-1]
        new = [pltpu.roll(n_refs[g][0, kv], shift, axis=1) for kv in range(2)]
        for e in range(heads_per_step):
            rows = slice(e * HEAD_DIM, (e + 1) * HEAD_DIM)
            qh = q_refs[g][i, :, rows].astype(BF16)
            s_c = _dot(qh, c_ref[i, 0, rows, :].astype(BF16)) + bc_refs[g][0, e]
            s_n = _dot(qh, new[0][rows, :].astype(BF16)) + bn_refs[g][0, e]
            m = jnp.maximum(jnp.max(s_c, axis=-1, keepdims=True), jnp.max(s_n, axis=-1, keepdims=True))
            p_c = jnp.exp(s_c - m)
            p_n = jnp.exp(s_n - m)
            l = jnp.sum(p_c, axis=-1, keepdims=True) + jnp.sum(p_n, axis=-1, keepdims=True)
            o = (_dot_nt(p_c.astype(BF16), c_ref[i, 1, rows, :].astype(BF16))
                 + _dot_nt(p_n.astype(BF16), new[1][rows, :].astype(BF16)))
            per_head[e].append((o / l, m + jnp.log(l)))
        for kv in range(2):
            rolled = pltpu.roll(c_ref[i, kv], length - t_new, axis=1)
            if length > LANES:
                oc_refs[g][i, kv, :, 0:length - LANES] = rolled[:, 0:length - LANES]
            oc_refs[g][i, kv, :, length - LANES:length] = jnp.where(
                lane >= LANES - t_new, new[kv], rolled[:, length - LANES:length])
    for e in range(heads_per_step):
        os_, lses = zip(*per_head[e])
        o_ref[i, :, e * HEAD_DIM:(e + 1) * HEAD_DIM] = _combine_groups(list(os_), list(lses))


def _native_view(cache):
    depth, nseq, length = cache.shape[:3]
    return cache.transpose(0, 1, 3, 4, 5, 2).reshape(depth * nseq, 2, ATT_WIDTH, length)


def _from_native(x, depth, nseq):
    length = x.shape[-1]
    return x.reshape(depth, nseq, 2, N_SLOTS, HEAD_DIM, length).transpose(0, 1, 5, 2, 3, 4)


def kernel(x_prompt, x_sample, c_prompt, c_sample, cache_kv_g0, cache_kv_g1, cache_kv_g2, state_conv, w_ada, b_ada, w_in, b_in, w_dw, b_dw, conv_ln_g, conv_ln_b, w_pc, b_pc, w_pa, b_pa, w_out, b_out, ln1_g, ln1_b, w_gate, w_up, w_down, ln2_g, ln2_b):
    depth = w_in.shape[0]
    bsz, seq, d = x_prompt.shape
    nseq, t_new, _ = x_sample.shape
    alpha = (2 * depth) ** 0.25
    tm = 512
    tm_ffn = 256
    m_s = nseq * t_new

    mod_all = _adaln(jnp.concatenate([c_prompt, c_sample], axis=0), w_ada, b_ada)
    caches_n = [_native_view(c) for c in (cache_kv_g0, cache_kv_g1, cache_kv_g2)]
    state_n = state_conv.transpose(0, 2, 1, 3)

    vec = lambda v: v.reshape(depth, 1, -1)
    w_in_bf = w_in.astype(BF16)
    b_in3 = vec(b_in)
    conv_w = [w_dw, vec(b_dw), vec(conv_ln_g), vec(conv_ln_b)]
    merge_w = [w_pc.astype(BF16), vec(b_pc), w_pa.astype(BF16), vec(b_pa),
               w_out.astype(BF16), vec(b_out), vec(ln1_g), vec(ln1_b)]
    ffn_w = [w_gate.astype(BF16), w_up.astype(BF16), w_down.astype(BF16), vec(ln2_g), vec(ln2_b)]

    xp = x_prompt
    xs = x_sample.transpose(1, 0, 2).reshape(1, m_s, d)
    conv_prompt, conv_sample = [], []
    cache_outs = None
    kv_nat = None
    for l in range(depth):
        mod_p = mod_all[l, :bsz].reshape(bsz, 1, MOD_WIDTH)
        mod_s = jnp.tile(mod_all[l, bsz:], (t_new, 1)).reshape(1, m_s, MOD_WIDTH)
        mod_sq = jnp.repeat(mod_all[l, bsz:], t_new, axis=0).reshape(1, m_s, MOD_WIDTH)

        xs_q = xs.reshape(t_new, nseq, d).transpose(1, 0, 2).reshape(1, m_s, d)
        glu_s, gc_s, ga_s, q_s, kvt_new = _inproj_s(xs, mod_s, xs_q, mod_sq, w_in_bf, b_in3, l)
        q8 = jnp.pad(q_s.reshape(t_new, nseq, N_GROUPS * ATT_WIDTH).transpose(1, 0, 2),
                     ((0, 0), (0, Q_ROWS - t_new), (0, 0)))

        glu, gate_c, gate_a, *qx = _inproj_a(xp, mod_p, w_in_bf, b_in3, l, tm)
        kvb, kv_nat, ycv = _inproj_b(xp, mod_p, w_in_bf, b_in3, glu, conv_w, l, depth, kv_nat, tm)
        conv_prompt.append(glu[:, seq - (CONV_WIDTH - 1):, :])
        os_, lses = [], []
        for g in range(N_GROUPS):
            o, lse = _prompt_attention(qx[g], kvb[g], DILATIONS[g])
            os_.append(o)
            lses.append(lse)
        x1 = _post_a(xp, mod_p, ycv, os_, lses, gate_c, gate_a, merge_w, l, tm, alpha)
        xp, o8, cache_outs = _post_b(x1, mod_p, ffn_w, l, tm_ffn, alpha,
                                     q8, caches_n, kvt_new, cache_outs, nseq, t_new)

        o_s = o8[:, :t_new].transpose(1, 0, 2).reshape(1, m_s, ATT_WIDTH)
        xs, nconv = _post_s(xs, mod_s, state_n, l, glu_s, o_s, gc_s, ga_s, conv_w + merge_w + ffn_w,
                            nseq, t_new, alpha)
        conv_sample.append(nconv[0])

    kv_p = [_from_native(kv_nat[g].reshape(depth * bsz, 2, ATT_WIDTH, -1), depth, bsz)
            for g in range(N_GROUPS)]
    kv_s = [_from_native(cache_outs[g], depth, nseq) for g in range(N_GROUPS)]
    return (xp, xs.reshape(t_new, nseq, d).transpose(1, 0, 2),
            kv_p[0], kv_p[1], kv_p[2], jnp.stack(conv_prompt),
            kv_s[0], kv_s[1], kv_s[2], jnp.stack(conv_sample).transpose(0, 2, 1, 3))
```

```python
import functools

import numpy as np
import jax
import jax.numpy as jnp
from jax import lax
from jax.experimental import pallas as pl
from jax.experimental.pallas import tpu as pltpu

D_MODEL = 1024
CONV_CH = 512
CONV_WIDTH = 31
N_SLOTS = 8
HEAD_DIM = 64
ATT_WIDTH = N_SLOTS * HEAD_DIM
WINDOWS = (128, 512, 2048)
DILATIONS = (1, 4, 16)
N_GROUPS = 3
SPAN = 128
BLOCK = 128
LN_EPS = 1e-5
QKV_OFF = 2 * CONV_CH
GATE_OFF = QKV_OFF + 3 * N_GROUPS * ATT_WIDTH
IN_WIDTH = GATE_OFF + 2 * D_MODEL
KV_WIDTH = 2 * ATT_WIDTH
MOD_WIDTH = 6 * D_MODEL
LANES = 128
NEG = -1e30
Q_ROWS = 8
CONV_HALO = 32
VMEM_LIMIT = 56 * 1024 * 1024

F32 = jnp.float32
BF16 = jnp.bfloat16
GATE_DTYPE = jnp.bfloat16

_NT = (((1,), (1,)), ((), ()))


def _q_col(g):
    return QKV_OFF + g * ATT_WIDTH


def _k_col(g):
    return QKV_OFF + (N_GROUPS + g) * ATT_WIDTH


def _v_col(g):
    return QKV_OFF + (2 * N_GROUPS + g) * ATT_WIDTH


def _dot(a, b):
    return jnp.dot(a, b, preferred_element_type=F32)


def _dot_nt(a, b):
    return lax.dot_general(a, b, _NT, preferred_element_type=F32)


def _sigmoid(x):
    return 1.0 / (1.0 + jnp.exp(-x))


def _silu(x):
    return x * _sigmoid(x)


def _layer_norm(x, g, b):
    mu = jnp.mean(x, axis=-1, keepdims=True)
    xc = x - mu
    var = jnp.mean(xc * xc, axis=-1, keepdims=True)
    return xc * lax.rsqrt(var + LN_EPS) * g + b


def _mod_slice(m, i):
    return m[:, i * D_MODEL:(i + 1) * D_MODEL]


def _const_spec(shape):
    nd = len(shape)
    return pl.BlockSpec(shape, lambda *_: (0,) * nd, pipeline_mode=pl.Buffered(1))


def _layer_spec(arr, layer):
    nd = arr.ndim
    return pl.BlockSpec((1,) + arr.shape[1:], lambda *_: (layer,) + (0,) * (nd - 1),
                        pipeline_mode=pl.Buffered(1))


def _params(n_axes):
    return pltpu.CompilerParams(dimension_semantics=("arbitrary",) * n_axes,
                                vmem_limit_bytes=VMEM_LIMIT)


def _adaln_kernel(c_ref, w_ref, b_ref, o_ref):
    c = c_ref[...]
    s = _silu(c).astype(BF16)
    o_ref[0] = _dot(s, w_ref[0].astype(BF16)) + b_ref[0]


def _adaln(c_all, w_ada, b_ada):
    depth, d, width = w_ada.shape
    rows = c_all.shape[0]
    tn = 1024
    return pl.pallas_call(
        _adaln_kernel,
        grid=(depth, width // tn),
        in_specs=[pl.BlockSpec((rows, d), lambda l, j: (0, 0)),
                  pl.BlockSpec((1, d, tn), lambda l, j: (l, 0, j)),
                  pl.BlockSpec((1, 1, tn), lambda l, j: (l, 0, j))],
        out_specs=pl.BlockSpec((1, rows, tn), lambda l, j: (l, 0, j)),
        out_shape=jax.ShapeDtypeStruct((depth, rows, width), F32),
        compiler_params=_params(2),
        name="adaln",
    )(c_all, w_ada, b_ada.reshape(depth, 1, width))


def _modulated(x_ref, mod_ref, shift_i, scale_i):
    m = mod_ref[0]
    x = x_ref[0]
    return (x * (1.0 + _mod_slice(m, scale_i)) + _mod_slice(m, shift_i)).astype(BF16)


def _proj(u, w_ref, b_ref, c0, width):
    return _dot(u, w_ref[0, :, c0:c0 + width]) + b_ref[0, :, c0:c0 + width]


def _glu_gates_q(u, w_ref, b_ref):
    glu = _proj(u, w_ref, b_ref, 0, CONV_CH) * _sigmoid(_proj(u, w_ref, b_ref, CONV_CH, CONV_CH))
    gate_c = _sigmoid(_proj(u, w_ref, b_ref, GATE_OFF, D_MODEL))
    gate_a = _sigmoid(_proj(u, w_ref, b_ref, GATE_OFF + D_MODEL, D_MODEL))
    qs = [_proj(u, w_ref, b_ref, _q_col(g), ATT_WIDTH) * (HEAD_DIM ** -0.5) for g in range(N_GROUPS)]
    return glu, gate_c, gate_a, qs


def _residue_rows(t, scr_ref, slot, dil):
    if dil == 1:
        return [t]
    rows = t.shape[0]
    scr_ref[slot] = t
    return [scr_ref[slot, pl.ds(r, rows // dil, stride=dil), :] for r in range(dil)]


def _inproj_a_kernel(x_ref, mod_ref, w_ref, b_ref, glu_ref, gc_ref, ga_ref, q0_ref, q1_ref, q2_ref,
                     scr_ref):
    u = _modulated(x_ref, mod_ref, 0, 1)
    glu, gate_c, gate_a, qs = _glu_gates_q(u, w_ref, b_ref)
    glu_ref[0] = glu
    gc_ref[0] = gate_c.astype(gc_ref.dtype)
    ga_ref[0] = gate_a.astype(ga_ref.dtype)
    for g, (q, q_ref) in enumerate(zip(qs, (q0_ref, q1_ref, q2_ref))):
        dil = DILATIONS[g]
        for p in range(ATT_WIDTH // LANES):
            parts = _residue_rows(q[:, p * LANES:(p + 1) * LANES], scr_ref, p, dil)
            for r, t in enumerate(parts):
                lane = lax.broadcasted_iota(jnp.int32, t.shape, 1)
                q_ref[0, r, :, (2 * p) * LANES:(2 * p + 1) * LANES] = jnp.where(lane < HEAD_DIM, t, 0.0).astype(BF16)
                q_ref[0, r, :, (2 * p + 1) * LANES:(2 * p + 2) * LANES] = jnp.where(lane >= HEAD_DIM, t, 0.0).astype(BF16)


def _residue_spec(tm, dil, width):
    return pl.BlockSpec((1, dil, tm // dil, width), lambda b, i: (b, 0, i, 0))


def _inproj_a(x, mod, w_in, b_in, layer, tm):
    bsz, s, d = x.shape
    nt = s // tm
    tile = lambda width: pl.BlockSpec((1, tm, width), lambda b, i: (b, i, 0))
    out_shape = ([jax.ShapeDtypeStruct((bsz, s, CONV_CH), F32),
                  jax.ShapeDtypeStruct((bsz, s, d), GATE_DTYPE),
                  jax.ShapeDtypeStruct((bsz, s, d), GATE_DTYPE)]
                 + [jax.ShapeDtypeStruct((bsz, dil, s // dil, 2 * ATT_WIDTH), BF16) for dil in DILATIONS])
    return pl.pallas_call(
        _inproj_a_kernel,
        grid=(bsz, nt),
        in_specs=[tile(d),
                  pl.BlockSpec((1, 1, MOD_WIDTH), lambda b, i: (b, 0, 0)),
                  _layer_spec(w_in, layer), _layer_spec(b_in, layer)],
        out_specs=[tile(CONV_CH), tile(d), tile(d)]
                  + [_residue_spec(tm, dil, 2 * ATT_WIDTH) for dil in DILATIONS],
        out_shape=out_shape,
        scratch_shapes=[pltpu.VMEM((ATT_WIDTH // LANES, tm, LANES), F32)],
        compiler_params=_params(2),
        name="inproj_a",
    )(x, mod, w_in, b_in)


def _conv_ln_silu(win_ref, r0, rb, wdw_ref, bdw_ref, cg_ref, cb_ref, out_ref):
    off = CONV_HALO - (CONV_WIDTH - 1)
    accs = []
    for c in range(CONV_CH // LANES):
        cols = slice(c * LANES, (c + 1) * LANES)
        acc = jnp.zeros((rb, LANES), F32) + bdw_ref[0, :, cols]
        for j in range(CONV_WIDTH):
            acc = acc + win_ref[c, pl.ds(r0 + off + j, rb, stride=1), :] * wdw_ref[0, j:j + 1, cols]
        accs.append(acc)
    y = _silu(_layer_norm(jnp.concatenate(accs, axis=1), cg_ref[0], cb_ref[0]))
    out_ref[0, r0:r0 + rb, :] = y.astype(out_ref.dtype)
    return y


def _ordering_zero(v):
    bits = pltpu.bitcast(v[:8, :LANES].astype(F32), jnp.uint32)
    zero = lax.shift_right_logical(lax.shift_right_logical(bits, jnp.uint32(16)), jnp.uint32(16))
    return zero.astype(F32).astype(BF16)


def _inproj_b_kernel(*refs, tm, seq, aliased):
    (x_ref, mod_ref, w_ref, b_ref, glu_ref, halo_ref,
     wdw_ref, bdw_ref, cg_ref, cb_ref) = refs[:10]
    n_in = 10 + (N_GROUPS if aliased else 0)
    kv_refs = refs[n_in:n_in + N_GROUPS]
    nat_refs = refs[n_in + N_GROUPS:n_in + 2 * N_GROUPS]
    ycv_ref, scr_ref, win_ref, keep_ref = refs[n_in + 2 * N_GROUPS:]
    ti = pl.program_id(1)
    halo = halo_ref[0]
    halo = jnp.where(ti > 0, halo, jnp.zeros_like(halo))
    glu = glu_ref[0]
    for c in range(CONV_CH // LANES):
        win_ref[c, 0:CONV_HALO, :] = halo[:, c * LANES:(c + 1) * LANES]
        win_ref[c, CONV_HALO:CONV_HALO + tm, :] = glu[:, c * LANES:(c + 1) * LANES]
    u = _modulated(x_ref, mod_ref, 0, 1)
    wins = [min(w, seq) for w in WINDOWS]
    every_tile = [w >= seq for w in wins]

    def project(g, half, after):
        kv_ref = kv_refs[g]
        lhs = u if after is None else u + jnp.tile(_ordering_zero(after), (tm // 8, u.shape[1] // LANES))
        t = _proj(lhs, w_ref, b_ref, (_k_col(g), _v_col(g))[half], ATT_WIDTH)
        for p in range(ATT_WIDTH // LANES):
            parts = _residue_rows(t[:, p * LANES:(p + 1) * LANES], scr_ref, half * 4 + p, DILATIONS[g])
            for r, part in enumerate(parts):
                cols = slice(half * ATT_WIDTH + p * LANES, half * ATT_WIDTH + (p + 1) * LANES)
                kv_ref[0, r, :, cols] = part.astype(BF16)
        if every_tile[g]:
            nat_refs[g][0, half * ATT_WIDTH:(half + 1) * ATT_WIDTH, :] = t.T
        else:
            keep_ref[2 * g + half] = t

    rb = 64
    conv_blocks = list(range(0, tm, rb))
    projections = [(g, half) for g in range(N_GROUPS) for half in range(2)]
    prev = None
    for i in range(max(len(conv_blocks), len(projections))):
        if i < len(projections):
            project(*projections[i], prev)
        if i < len(conv_blocks):
            prev = _conv_ln_silu(win_ref, conv_blocks[i], rb, wdw_ref, bdw_ref, cg_ref, cb_ref, ycv_ref)
    for g, n_ref in enumerate(nat_refs):
        if every_tile[g]:
            continue
        win = wins[g]
        rows = min(win, tm)
        cond = (ti >= (seq - win) // tm) if win >= tm else (ti == seq // tm - 1)

        @pl.when(cond)
        def _(g=g, n_ref=n_ref, rows=rows):
            for half in range(2):
                n_ref[0, half * ATT_WIDTH:(half + 1) * ATT_WIDTH, :] = keep_ref[2 * g + half, tm - rows:tm, :].T


def _inproj_b(x, mod, w_in, b_in, glu, conv_w, layer, depth, prev_nat, tm):
    bsz, s, d = x.shape
    nt = s // tm
    assert tm % CONV_HALO == 0
    tile = lambda width: pl.BlockSpec((1, tm, width), lambda b, i: (b, i, 0))
    halo_spec = pl.BlockSpec(
        (1, CONV_HALO, CONV_CH), lambda b, i: (b, jnp.maximum(i * (tm // CONV_HALO) - 1, 0), 0))
    nat_specs, nat_shapes = [], []
    for g in range(N_GROUPS):
        win = min(WINDOWS[g], s)
        assert win % tm == 0 or tm % win == 0
        nat_shapes.append(jax.ShapeDtypeStruct((depth * bsz, KV_WIDTH, win), F32))
        if win >= tm:
            first = (s - win) // tm
            nat_specs.append(pl.BlockSpec(
                (1, KV_WIDTH, tm),
                lambda b, i, first=first: (layer * bsz + b, 0, jnp.maximum(i - first, 0))))
        else:
            nat_specs.append(pl.BlockSpec((1, KV_WIDTH, win), lambda b, i: (layer * bsz + b, 0, 0)))
    args = [x, mod, w_in, b_in, glu, glu, *conv_w]
    in_specs = [tile(d), pl.BlockSpec((1, 1, MOD_WIDTH), lambda b, i: (b, 0, 0)),
                _layer_spec(w_in, layer), _layer_spec(b_in, layer),
                tile(CONV_CH), halo_spec] + [_layer_spec(w, layer) for w in conv_w]
    aliases = {}
    if prev_nat is not None:
        for g in range(N_GROUPS):
            in_specs.append(pl.BlockSpec(memory_space=pl.ANY))
            aliases[len(args)] = N_GROUPS + g
            args.append(prev_nat[g])
    outs = pl.pallas_call(
        functools.partial(_inproj_b_kernel, tm=tm, seq=s, aliased=prev_nat is not None),
        grid=(bsz, nt),
        in_specs=in_specs,
        out_specs=[_residue_spec(tm, dil, KV_WIDTH) for dil in DILATIONS] + nat_specs + [tile(CONV_CH)],
        out_shape=[jax.ShapeDtypeStruct((bsz, dil, s // dil, KV_WIDTH), BF16) for dil in DILATIONS]
                  + nat_shapes + [jax.ShapeDtypeStruct((bsz, s, CONV_CH), BF16)],
        scratch_shapes=[pltpu.VMEM((KV_WIDTH // LANES, tm, LANES), F32),
                        pltpu.VMEM((CONV_CH // LANES, CONV_HALO + tm, LANES), F32),
                        pltpu.VMEM((2 * N_GROUPS, tm, ATT_WIDTH), F32)],
        input_output_aliases=aliases,
        compiler_params=_params(2),
        name="inproj_b",
    )(*args)
    return outs[:N_GROUPS], outs[N_GROUPS:2 * N_GROUPS], outs[2 * N_GROUPS]


def _inproj_s_kernel(x_ref, mod_ref, xq_ref, modq_ref, w_ref, b_ref,
                     glu_ref, gc_ref, ga_ref, q_ref, kvt_ref):
    u = _modulated(x_ref, mod_ref, 0, 1)
    glu, gate_c, gate_a, qs = _glu_gates_q(u, w_ref, b_ref)
    glu_ref[0] = glu
    gc_ref[0] = gate_c.astype(gc_ref.dtype)
    ga_ref[0] = gate_a.astype(ga_ref.dtype)
    for g, q in enumerate(qs):
        q_ref[0, :, g * ATT_WIDTH:(g + 1) * ATT_WIDTH] = q
    uq = _modulated(xq_ref, modq_ref, 0, 1)
    for g in range(N_GROUPS):
        for half, c0 in enumerate((_k_col(g), _v_col(g))):
            r0 = g * KV_WIDTH + half * ATT_WIDTH
            kvt_ref[r0:r0 + ATT_WIDTH, :] = _proj(uq, w_ref, b_ref, c0, ATT_WIDTH).T


def _inproj_s(x, mod, xq, modq, w_in, b_in, layer):
    _, m, d = x.shape
    full = lambda *shape: _const_spec(shape)
    whole = lambda *shape: pl.BlockSpec(shape, lambda i: (0,) * len(shape))
    return pl.pallas_call(
        _inproj_s_kernel,
        in_specs=[full(1, m, d), full(1, m, MOD_WIDTH), full(1, m, d), full(1, m, MOD_WIDTH),
                  _layer_spec(w_in, layer), _layer_spec(b_in, layer)],
        out_specs=[whole(1, m, CONV_CH), whole(1, m, d), whole(1, m, d),
                   whole(1, m, N_GROUPS * ATT_WIDTH), whole(N_GROUPS * KV_WIDTH, m)],
        out_shape=[jax.ShapeDtypeStruct((1, m, CONV_CH), F32),
                   jax.ShapeDtypeStruct((1, m, d), GATE_DTYPE),
                   jax.ShapeDtypeStruct((1, m, d), GATE_DTYPE),
                   jax.ShapeDtypeStruct((1, m, N_GROUPS * ATT_WIDTH), F32),
                   jax.ShapeDtypeStruct((N_GROUPS * KV_WIDTH, m), F32)],
        grid=(1,),
        compiler_params=_params(1),
        name="inproj_s",
    )(x, mod, xq, modq, w_in, b_in)


def _alibi_slopes():
    return (2.0 ** (-8.0 * (np.arange(N_SLOTS) + 1) / N_SLOTS)).astype(np.float32)


def _prompt_bias(dil, key_blocks):
    qi = np.arange(BLOCK)[:, None]
    kj = np.arange(key_blocks * BLOCK)[None, :]
    tables = []
    for lead in (0, key_blocks - 1):
        rel = lead * BLOCK + qi - kj
        valid = (rel >= 0) & (rel <= SPAN)
        bias = -_alibi_slopes()[:, None, None] * (rel * dil).astype(np.float32)[None]
        tables.append(np.where(valid[None], bias, np.float32(NEG)))
    return np.stack(tables).astype(np.float32)


def _attn_kernel(q_ref, kv_ref, bias_ref, o_ref, lse_ref, s_scr, m_scr, *, nb, n_units):
    kw = bias_ref.shape[-1]
    lane = lax.broadcasted_iota(jnp.int32, (BLOCK, LANES), 1)
    pick = lambda a, b: jnp.where(lane < HEAD_DIM, a, b)

    def place(u):
        if isinstance(u, int):
            rr, c = divmod(u, nb)
            r0 = c * BLOCK
            k0 = max(r0 - (kw - BLOCK), 0)
            return rr, r0, k0, (0 if c == 0 else 1)
        rr = u // nb
        c = u % nb
        r0 = pl.multiple_of(c * BLOCK, BLOCK)
        k0 = pl.multiple_of(jnp.maximum(r0 - (kw - BLOCK), 0), BLOCK)
        return rr, r0, k0, jnp.minimum(c, 1)

    ones = jnp.ones((kw, LANES), BF16)

    def scores(u, slot):
        rr, r0, k0, tbl = place(u)
        for p in range(ATT_WIDTH // LANES):
            kp = kv_ref[0, rr, pl.ds(k0, kw), p * LANES:(p + 1) * LANES]
            q2 = q_ref[0, rr, pl.ds(r0, BLOCK), 2 * p * LANES:(2 * p + 2) * LANES]
            q2 = jnp.concatenate([q2[:, :LANES], q2[:, LANES:]], axis=0)
            bias2 = jnp.concatenate([bias_ref[tbl, 2 * p], bias_ref[tbl, 2 * p + 1]], axis=0)
            s = _dot_nt(q2, kp) + bias2
            s_scr[slot, p] = s
            m_scr[slot, p] = jnp.max(s, axis=-1, keepdims=True)

    def finish(u, slot):
        rr, r0, k0, _ = place(u)
        for p in range(ATT_WIDTH // LANES):
            vp = kv_ref[0, rr, pl.ds(k0, kw), ATT_WIDTH + p * LANES:ATT_WIDTH + (p + 1) * LANES]
            m = m_scr[slot, p]
            pexp = jnp.exp(s_scr[slot, p] - m).astype(BF16)
            pv = _dot(pexp, jnp.concatenate([vp, ones], axis=1))
            l2 = pick(pv[:BLOCK, LANES:], pv[BLOCK:, LANES:])
            cols = slice(p * LANES, (p + 1) * LANES)
            o_ref[0, rr, pl.ds(r0, BLOCK), cols] = pick(pv[:BLOCK, :LANES], pv[BLOCK:, :LANES]) / l2
            lse_ref[0, rr, pl.ds(r0, BLOCK), cols] = pick(m[:BLOCK], m[BLOCK:]) + jnp.log(l2)

    scores(0, 0)

    def body(u, carry):
        slot = u % 2
        finish(u, slot)
        scores(u + 1, 1 - slot)
        return carry
    lax.fori_loop(0, n_units - 1, body, 0)
    finish(n_units - 1, (n_units - 1) % 2)


def _prompt_attention(qx, kvb, dil):
    bsz, _, n, _ = qx.shape
    assert n % BLOCK == 0
    nb = n // BLOCK
    nres = max(1, min(dil, 16 // nb))
    assert dil % nres == 0
    bias = jnp.asarray(_prompt_bias(dil, min(nb, 2)))
    sub = lambda width: pl.BlockSpec((1, nres, n, width), lambda b, r: (b, r, 0, 0))
    return pl.pallas_call(
        functools.partial(_attn_kernel, nb=nb, n_units=nres * nb),
        grid=(bsz, dil // nres),
        in_specs=[sub(2 * ATT_WIDTH), sub(KV_WIDTH), _const_spec(bias.shape)],
        out_specs=[sub(ATT_WIDTH), sub(ATT_WIDTH)],
        out_shape=[jax.ShapeDtypeStruct((bsz, dil, n, ATT_WIDTH), F32)] * 2,
        scratch_shapes=[pltpu.VMEM((2, N_SLOTS // 2, 2 * BLOCK, bias.shape[-1]), F32),
                        pltpu.VMEM((2, N_SLOTS // 2, 2 * BLOCK, 1), F32)],
        compiler_params=_params(2),
        name=f"attn_d{dil}",
    )(qx, kvb, bias)


def _combine_groups(os_, lses):
    m = functools.reduce(jnp.maximum, lses)
    ws = [jnp.exp(l - m) for l in lses]
    den = functools.reduce(lambda a, b: a + b, ws)
    num = functools.reduce(lambda a, b: a + b, [w * o for w, o in zip(ws, os_)])
    return num / den


def _merge_to_ln1(x, m, ybuf, o_att, gate_c, gate_a,
                  wpc_ref, bpc_ref, wpa_ref, bpa_ref, wout_ref, bout_ref, g1_ref, b1_ref, alpha):
    y_conv = _dot(ybuf, wpc_ref[0]) + bpc_ref[0]
    y_att = _dot(o_att.astype(BF16), wpa_ref[0]) + bpa_ref[0]
    y = (gate_c * y_conv + gate_a * y_att).astype(BF16)
    y = _dot(y, wout_ref[0]) + bout_ref[0]
    return _layer_norm(alpha * x + (1.0 + _mod_slice(m, 2)) * y, g1_ref[0], b1_ref[0])


def _ffn_to_ln2(x1, m, wg_ref, wu_ref, wd_ref, g2_ref, b2_ref, alpha, chunk):
    u2 = (x1 * (1.0 + _mod_slice(m, 4)) + _mod_slice(m, 3)).astype(BF16)
    d_ff = wg_ref.shape[-1]
    h = None
    for c0 in range(0, d_ff, chunk):
        c1 = min(c0 + chunk, d_ff)
        a = (_silu(_dot(u2, wg_ref[0, :, c0:c1])) * _dot(u2, wu_ref[0, :, c0:c1])).astype(BF16)
        part = _dot(a, wd_ref[0, c0:c1, :])
        h = part if h is None else h + part
    return _layer_norm(alpha * x1 + (1.0 + _mod_slice(m, 5)) * h, g2_ref[0], b2_ref[0])


def _token_order(ref, scr_ref, slot, p, dil):
    cols = slice(p * LANES, (p + 1) * LANES)
    if dil == 1:
        return ref[0, 0, :, cols]
    n = ref.shape[2]
    for r in range(dil):
        scr_ref[slot, pl.ds(r, n, stride=dil), :] = ref[0, r, :, cols]
    return scr_ref[slot]


def _post_a_kernel(x_ref, mod_ref, ycv_ref, o0_ref, o1_ref, o2_ref, l0_ref, l1_ref, l2_ref,
                   gc_ref, ga_ref, wpc_ref, bpc_ref, wpa_ref, bpa_ref, wout_ref, bout_ref,
                   g1_ref, b1_ref, out_ref, scr_ref, *, alpha):
    o_refs = (o0_ref, o1_ref, o2_ref)
    l_refs = (l0_ref, l1_ref, l2_ref)
    chunks = []
    for p in range(ATT_WIDTH // LANES):
        os_ = [_token_order(o_refs[g], scr_ref, 2 * g, p, DILATIONS[g]) for g in range(N_GROUPS)]
        lses = [_token_order(l_refs[g], scr_ref, 2 * g + 1, p, DILATIONS[g]) for g in range(N_GROUPS)]
        chunks.append(_combine_groups(os_, lses).astype(BF16))
    o_att = jnp.concatenate(chunks, axis=1)
    out_ref[0] = _merge_to_ln1(x_ref[0], mod_ref[0], ycv_ref[0], o_att,
                               gc_ref[0].astype(F32), ga_ref[0].astype(F32),
                               wpc_ref, bpc_ref, wpa_ref, bpa_ref, wout_ref, bout_ref,
                               g1_ref, b1_ref, alpha)


def _post_a(x, mod, ycv, os_, lses, gate_c, gate_a, wts, layer, tm, alpha):
    bsz, s, d = x.shape
    nt = s // tm
    tile = lambda width: pl.BlockSpec((1, tm, width), lambda b, i: (b, i, 0))
    res_specs = [_residue_spec(tm, dil, ATT_WIDTH) for dil in DILATIONS]
    return pl.pallas_call(
        functools.partial(_post_a_kernel, alpha=alpha),
        grid=(bsz, nt),
        in_specs=([tile(d), pl.BlockSpec((1, 1, MOD_WIDTH), lambda b, i: (b, 0, 0)), tile(CONV_CH)]
                  + res_specs + res_specs + [tile(d), tile(d)]
                  + [_layer_spec(w, layer) for w in wts]),
        out_specs=tile(d),
        out_shape=jax.ShapeDtypeStruct((bsz, s, d), F32),
        scratch_shapes=[pltpu.VMEM((2 * N_GROUPS, tm, LANES), F32)],
        compiler_params=_params(2),
        name="post_a",
    )(x, mod, ycv, *os_, *lses, gate_c, gate_a, *wts)


def _post_b_kernel(*refs, alpha, chunk, t_new, nsq, nblk):
    x_ref, mod_ref, wg_ref, wu_ref, wd_ref, g2_ref, b2_ref = refs[:7]
    q_refs, c_refs, n_refs = refs[7:10], refs[10:13], refs[13:16]
    bc_refs, bn_refs = refs[16:19], refs[19:22]
    out_ref, o_ref = refs[-5:-3]
    oc_refs = refs[-3:]
    out_ref[0] = _ffn_to_ln2(x_ref[0], mod_ref[0], wg_ref, wu_ref, wd_ref, g2_ref, b2_ref, alpha, chunk)
    step = pl.program_id(0) * pl.num_programs(1) + pl.program_id(1)
    first_seq = (step % nblk) * nsq
    for i in range(nsq):
        _sample_unit(q_refs, c_refs, n_refs, bc_refs, bn_refs, o_ref, oc_refs, i, first_seq + i, t_new)


def _post_b(x1, mod, wts, layer, tm, alpha, q8, caches_n, kvt_new, prev_outs, nseq, t_new):
    bsz, s, d = x1.shape
    nt = s // tm
    hsteps = ATT_WIDTH // LANES
    steps = bsz * nt
    assert (hsteps * nseq) % steps == 0 and nseq * t_new == LANES
    nsq = hsteps * nseq // steps
    nblk = nseq // nsq
    side = lambda f: (lambda b, i: f(*divmod(b * nt + i, nblk)))
    tables = [_sample_bias(g, caches_n[g].shape[-1], t_new) for g in range(N_GROUPS)]
    bc = [jnp.asarray(tc.reshape(hsteps, 2, Q_ROWS, -1)) for tc, _ in tables]
    bn = [jnp.asarray(tn.reshape(hsteps, 2, Q_ROWS, LANES)) for _, tn in tables]
    kvt4 = kvt_new.reshape(N_GROUPS, 2, ATT_WIDTH, nseq * t_new)
    tile = pl.BlockSpec((1, tm, d), lambda b, i: (b, i, 0))
    args = [x1, mod, *wts]
    in_specs = ([tile, pl.BlockSpec((1, 1, MOD_WIDTH), lambda b, i: (b, 0, 0))]
                + [_layer_spec(w, layer) for w in wts])
    for g in range(N_GROUPS):
        in_specs.append(pl.BlockSpec((nsq, Q_ROWS, LANES), side(lambda hc, blk, g=g: (blk, 0, g * hsteps + hc))))
        args.append(q8)
    cache_specs = [pl.BlockSpec((nsq, 2, LANES, c.shape[-1]),
                                side(lambda hc, blk: (layer * nblk + blk, 0, hc, 0))) for c in caches_n]
    in_specs += cache_specs
    args += list(caches_n)
    for g in range(N_GROUPS):
        in_specs.append(pl.BlockSpec((1, 2, LANES, LANES), side(lambda hc, blk, g=g: (g, 0, hc, 0))))
        args.append(kvt4)
    for tbl in bc + bn:
        in_specs.append(pl.BlockSpec((1,) + tbl.shape[1:], side(lambda hc, blk: (hc, 0, 0, 0))))
        args.append(tbl)
    aliases = {}
    if prev_outs is not None:
        for g in range(N_GROUPS):
            in_specs.append(pl.BlockSpec(memory_space=pl.ANY))
            aliases[len(args)] = 2 + g
            args.append(prev_outs[g])
    outs = pl.pallas_call(
        functools.partial(_post_b_kernel, alpha=alpha, chunk=1024, t_new=t_new, nsq=nsq, nblk=nblk),
        grid=(bsz, nt),
        in_specs=in_specs,
        out_specs=[tile, pl.BlockSpec((nsq, Q_ROWS, LANES), side(lambda hc, blk: (blk, 0, hc)))] + cache_specs,
        out_shape=[jax.ShapeDtypeStruct((bsz, s, d), F32),
                   jax.ShapeDtypeStruct((nseq, Q_ROWS, ATT_WIDTH), F32)]
                  + [jax.ShapeDtypeStruct(c.shape, F32) for c in caches_n],
        input_output_aliases=aliases,
        compiler_params=_params(2),
        name="post_b",
    )(*args)
    return outs[0], outs[1], outs[2:]


def _post_s_kernel(x_ref, mod_ref, state_ref, glu_ref, o_ref, gc_ref, ga_ref,
                   wdw_ref, bdw_ref, cg_ref, cb_ref,
                   wpc_ref, bpc_ref, wpa_ref, bpa_ref, wout_ref, bout_ref, g1_ref, b1_ref,
                   wg_ref, wu_ref, wd_ref, g2_ref, b2_ref,
                   out_ref, nconv_ref, ypre_ref, *, nseq, t_new, alpha, chunk):
    ctx = CONV_WIDTH - 1

    def slab(i):
        if i < ctx:
            return state_ref[0, i]
        return glu_ref[0, (i - ctx) * nseq:(i - ctx + 1) * nseq, :]

    for t in range(t_new):
        acc = jnp.zeros((nseq, CONV_CH), F32) + bdw_ref[0]
        for j in range(CONV_WIDTH):
            acc = acc + slab(t + j) * wdw_ref[0, j:j + 1, :]
        ypre_ref[t * nseq:(t + 1) * nseq, :] = acc
    for i in range(ctx):
        nconv_ref[0, i] = slab(i + t_new)
    ybuf = _silu(_layer_norm(ypre_ref[...], cg_ref[0], cb_ref[0])).astype(BF16)
    m = mod_ref[0]
    x1 = _merge_to_ln1(x_ref[0], m, ybuf, o_ref[0], gc_ref[0].astype(F32), ga_ref[0].astype(F32),
                       wpc_ref, bpc_ref, wpa_ref, bpa_ref, wout_ref, bout_ref, g1_ref, b1_ref, alpha)
    out_ref[0] = _ffn_to_ln2(x1, m, wg_ref, wu_ref, wd_ref, g2_ref, b2_ref, alpha, chunk)


def _post_s(x, mod, state_n, layer, glu, o_att, gate_c, gate_a, wts, nseq, t_new, alpha):
    _, m, d = x.shape
    ctx = CONV_WIDTH - 1
    full = lambda *shape: _const_spec(shape)
    whole = lambda *shape: pl.BlockSpec(shape, lambda i: (0,) * len(shape))
    return pl.pallas_call(
        functools.partial(_post_s_kernel, nseq=nseq, t_new=t_new, alpha=alpha, chunk=1024),
        in_specs=[full(1, m, d), full(1, m, MOD_WIDTH), _layer_spec(state_n, layer), full(1, m, CONV_CH),
                  full(1, m, ATT_WIDTH), full(1, m, d), full(1, m, d)]
                 + [_layer_spec(w, layer) for w in wts],
        out_specs=[whole(1, m, d), whole(1, ctx, nseq, CONV_CH)],
        out_shape=[jax.ShapeDtypeStruct((1, m, d), F32),
                   jax.ShapeDtypeStruct((1, ctx, nseq, CONV_CH), F32)],
        scratch_shapes=[pltpu.VMEM((m, CONV_CH), F32)],
        grid=(1,),
        compiler_params=_params(1),
        name="post_s",
    )(x, mod, state_n, glu, o_att, gate_c, gate_a, *wts)


def _sample_bias(g, length, t_new):
    dil = DILATIONS[g]
    slopes = _alibi_slopes()[:, None, None]
    t = np.arange(Q_ROWS)[:, None]
    p = np.arange(length)[None, :]
    dist = length + t - p
    valid = (dist % dil == 0) & (dist // dil >= 1) & (dist // dil <= SPAN) & (t < t_new)
    cache = np.where(valid[None], -slopes * dist.astype(np.float32)[None], np.float32(NEG))
    tp = np.arange(LANES)[None, :] - (LANES - t_new)
    dist_n = t - tp
    valid_n = (tp >= 0) & (dist_n >= 0) & (dist_n % dil == 0) & (dist_n // dil <= SPAN) & (t < t_new)
    new = np.where(valid_n[None], -slopes * dist_n.astype(np.float32)[None], np.float32(NEG))
    return cache.astype(np.float32), new.astype(np.float32)


def _sample_unit(q_refs, c_refs, n_refs, bc_refs, bn_refs, o_ref, oc_refs, i, seq, t_new):
    shift = (LANES - t_new) - t_new * seq
    lane = lax.broadcasted_iota(jnp.int32, (LANES, LANES), 1)
    heads_per_step = LANES // HEAD_DIM
    per_head = [[] for _ in range(heads_per_step)]
    for g in range(N_GROUPS):
        c_ref = c_refs[g]
        length = c_ref.shape[-1]
        new = [pltpu.roll(n_refs[g][0, kv], shift, axis=1) for kv in range(2)]
        for e in range(heads_per_step):
            rows = slice(e * HEAD_DIM, (e + 1) * HEAD_DIM)
            qh = q_refs[g][i, :, rows].astype(BF16)
            s_c = _dot(qh, c_ref[i, 0, rows, :].astype(BF16)) + bc_refs[g][0, e]
            s_n = _dot(qh, new[0][rows, :].astype(BF16)) + bn_refs[g][0, e]
            m = jnp.maximum(jnp.max(s_c, axis=-1, keepdims=True), jnp.max(s_n, axis=-1, keepdims=True))
            p_c = jnp.exp(s_c - m)
            p_n = jnp.exp(s_n - m)
            l = jnp.sum(p_c, axis=-1, keepdims=True) + jnp.sum(p_n, axis=-1, keepdims=True)
            o = (_dot_nt(p_c.astype(BF16), c_ref[i, 1, rows, :].astype(BF16))
                 + _dot_nt(p_n.astype(BF16), new[1][rows, :].astype(BF16)))
            per_head[e].append((o / l, m + jnp.log(l)))
        for kv in range(2):
            rolled = pltpu.roll(c_ref[i, kv], length - t_new, axis=1)
            if length > LANES:
                oc_refs[g][i, kv, :, 0:length - LANES] = rolled[:, 0:length - LANES]
            oc_refs[g][i, kv, :, length - LANES:length] = jnp.where(
                lane >= LANES - t_new, new[kv], rolled[:, length - LANES:length])
    for e in range(heads_per_step):
        os_, lses = zip(*per_head[e])
        o_ref[i, :, e * HEAD_DIM:(e + 1) * HEAD_DIM] = _combine_groups(list(os_), list(lses))


def _native_view(cache):
    depth, nseq, length = cache.shape[:3]
    return cache.transpose(0, 1, 3, 4, 5, 2).reshape(depth * nseq, 2, ATT_WIDTH, length)


def _from_native(x, depth, nseq):
    length = x.shape[-1]
    return x.reshape(depth, nseq, 2, N_SLOTS, HEAD_DIM, length).transpose(0, 1, 5, 2, 3, 4)


def kernel(x_prompt, x_sample, c_prompt, c_sample, cache_kv_g0, cache_kv_g1, cache_kv_g2, state_conv, w_ada, b_ada, w_in, b_in, w_dw, b_dw, conv_ln_g, conv_ln_b, w_pc, b_pc, w_pa, b_pa, w_out, b_out, ln1_g, ln1_b, w_gate, w_up, w_down, ln2_g, ln2_b):
    depth = w_in.shape[0]
    bsz, seq, d = x_prompt.shape
    nseq, t_new, _ = x_sample.shape
    alpha = (2 * depth) ** 0.25
    tm = 512
    tm_a = 1024
    tm_ffn = 256
    m_s = nseq * t_new

    mod_all = _adaln(jnp.concatenate([c_prompt, c_sample], axis=0), w_ada, b_ada)
    caches_n = [_native_view(c) for c in (cache_kv_g0, cache_kv_g1, cache_kv_g2)]
    state_n = state_conv.transpose(0, 2, 1, 3)

    vec = lambda v: v.reshape(depth, 1, -1)
    w_in_bf = w_in.astype(BF16)
    b_in3 = vec(b_in)
    conv_w = [w_dw, vec(b_dw), vec(conv_ln_g), vec(conv_ln_b)]
    merge_w = [w_pc.astype(BF16), vec(b_pc), w_pa.astype(BF16), vec(b_pa),
               w_out.astype(BF16), vec(b_out), vec(ln1_g), vec(ln1_b)]
    ffn_w = [w_gate.astype(BF16), w_up.astype(BF16), w_down.astype(BF16), vec(ln2_g), vec(ln2_b)]

    xp = x_prompt
    xs = x_sample.transpose(1, 0, 2).reshape(1, m_s, d)
    conv_prompt, conv_sample = [], []
    cache_outs = None
    kv_nat = None
    for l in range(depth):
        mod_p = mod_all[l, :bsz].reshape(bsz, 1, MOD_WIDTH)
        mod_s = jnp.tile(mod_all[l, bsz:], (t_new, 1)).reshape(1, m_s, MOD_WIDTH)
        mod_sq = jnp.repeat(mod_all[l, bsz:], t_new, axis=0).reshape(1, m_s, MOD_WIDTH)

        xs_q = xs.reshape(t_new, nseq, d).transpose(1, 0, 2).reshape(1, m_s, d)
        glu_s, gc_s, ga_s, q_s, kvt_new = _inproj_s(xs, mod_s, xs_q, mod_sq, w_in_bf, b_in3, l)
        q8 = jnp.pad(q_s.reshape(t_new, nseq, N_GROUPS * ATT_WIDTH).transpose(1, 0, 2),
                     ((0, 0), (0, Q_ROWS - t_new), (0, 0)))

        glu, gate_c, gate_a, *qx = _inproj_a(xp, mod_p, w_in_bf, b_in3, l, tm_a)
        kvb, kv_nat, ycv = _inproj_b(xp, mod_p, w_in_bf, b_in3, glu, conv_w, l, depth, kv_nat, tm)
        conv_prompt.append(glu[:, seq - (CONV_WIDTH - 1):, :])
        os_, lses = [], []
        for g in range(N_GROUPS):
            o, lse = _prompt_attention(qx[g], kvb[g], DILATIONS[g])
            os_.append(o)
            lses.append(lse)
        x1 = _post_a(xp, mod_p, ycv, os_, lses, gate_c, gate_a, merge_w, l, tm, alpha)
        xp, o8, cache_outs = _post_b(x1, mod_p, ffn_w, l, tm_ffn, alpha,
                                     q8, caches_n, kvt_new, cache_outs, nseq, t_new)

        o_s = o8[:, :t_new].transpose(1, 0, 2).reshape(1, m_s, ATT_WIDTH)
        xs, nconv = _post_s(xs, mod_s, state_n, l, glu_s, o_s, gc_s, ga_s, conv_w + merge_w + ffn_w,
                            nseq, t_new, alpha)
        conv_sample.append(nconv[0])

    kv_p = [_from_native(kv_nat[g].reshape(depth * bsz, 2, ATT_WIDTH, -1), depth, bsz)
            for g in range(N_GROUPS)]
    kv_s = [_from_native(cache_outs[g], depth, nseq) for g in range(N_GROUPS)]
    return (xp, xs.reshape(t_new, nseq, d).transpose(1, 0, 2),
            kv_p[0], kv_p[1], kv_p[2], jnp.stack(conv_prompt),
            kv_s[0], kv_s[1], kv_s[2], jnp.stack(conv_sample).transpose(0, 2, 1, 3))
```

```python
import functools

import numpy as np
import jax
import jax.numpy as jnp
from jax import lax
from jax.experimental import pallas as pl
from jax.experimental.pallas import tpu as pltpu

D_MODEL = 1024
CONV_CH = 512
CONV_WIDTH = 31
N_SLOTS = 8
HEAD_DIM = 64
ATT_WIDTH = N_SLOTS * HEAD_DIM
WINDOWS = (128, 512, 2048)
DILATIONS = (1, 4, 16)
N_GROUPS = 3
SPAN = 128
BLOCK = 128
LN_EPS = 1e-5
QKV_OFF = 2 * CONV_CH
GATE_OFF = QKV_OFF + 3 * N_GROUPS * ATT_WIDTH
IN_WIDTH = GATE_OFF + 2 * D_MODEL
KV_WIDTH = 2 * ATT_WIDTH
MOD_WIDTH = 6 * D_MODEL
LANES = 128
NEG = -1e30
Q_ROWS = 8
CONV_HALO = 32
VMEM_LIMIT = 56 * 1024 * 1024

F32 = jnp.float32
BF16 = jnp.bfloat16
GATE_DTYPE = jnp.bfloat16

_NT = (((1,), (1,)), ((), ()))


def _q_col(g):
    return QKV_OFF + g * ATT_WIDTH


def _k_col(g):
    return QKV_OFF + (N_GROUPS + g) * ATT_WIDTH


def _v_col(g):
    return QKV_OFF + (2 * N_GROUPS + g) * ATT_WIDTH


def _dot(a, b):
    return jnp.dot(a, b, preferred_element_type=F32)


def _dot_nt(a, b):
    return lax.dot_general(a, b, _NT, preferred_element_type=F32)


def _sigmoid(x):
    return 1.0 / (1.0 + jnp.exp(-x))


def _silu(x):
    return x * _sigmoid(x)


def _layer_norm(x, g, b):
    mu = jnp.mean(x, axis=-1, keepdims=True)
    xc = x - mu
    var = jnp.mean(xc * xc, axis=-1, keepdims=True)
    return xc * lax.rsqrt(var + LN_EPS) * g + b


def _mod_slice(m, i):
    return m[:, i * D_MODEL:(i + 1) * D_MODEL]


def _const_spec(shape):
    nd = len(shape)
    return pl.BlockSpec(shape, lambda *_: (0,) * nd, pipeline_mode=pl.Buffered(1))


def _layer_spec(arr, layer):
    nd = arr.ndim
    return pl.BlockSpec((1,) + arr.shape[1:], lambda *_: (layer,) + (0,) * (nd - 1),
                        pipeline_mode=pl.Buffered(1))


def _params(n_axes):
    return pltpu.CompilerParams(dimension_semantics=("arbitrary",) * n_axes,
                                vmem_limit_bytes=VMEM_LIMIT)


def _adaln_kernel(c_ref, w_ref, b_ref, o_ref):
    c = c_ref[...]
    s = _silu(c).astype(BF16)
    o_ref[0] = _dot(s, w_ref[0].astype(BF16)) + b_ref[0]


def _adaln(c_all, w_ada, b_ada):
    depth, d, width = w_ada.shape
    rows = c_all.shape[0]
    tn = 1024
    return pl.pallas_call(
        _adaln_kernel,
        grid=(depth, width // tn),
        in_specs=[pl.BlockSpec((rows, d), lambda l, j: (0, 0)),
                  pl.BlockSpec((1, d, tn), lambda l, j: (l, 0, j)),
                  pl.BlockSpec((1, 1, tn), lambda l, j: (l, 0, j))],
        out_specs=pl.BlockSpec((1, rows, tn), lambda l, j: (l, 0, j)),
        out_shape=jax.ShapeDtypeStruct((depth, rows, width), F32),
        compiler_params=_params(2),
        name="adaln",
    )(c_all, w_ada, b_ada.reshape(depth, 1, width))


def _modulated(x_ref, mod_ref, shift_i, scale_i):
    m = mod_ref[0]
    x = x_ref[0]
    return (x * (1.0 + _mod_slice(m, scale_i)) + _mod_slice(m, shift_i)).astype(BF16)


def _proj(u, w_ref, b_ref, c0, width):
    return _dot(u, w_ref[0, :, c0:c0 + width]) + b_ref[0, :, c0:c0 + width]


def _glu_gates_q(u, w_ref, b_ref):
    glu = _proj(u, w_ref, b_ref, 0, CONV_CH) * _sigmoid(_proj(u, w_ref, b_ref, CONV_CH, CONV_CH))
    gate_c = _sigmoid(_proj(u, w_ref, b_ref, GATE_OFF, D_MODEL))
    gate_a = _sigmoid(_proj(u, w_ref, b_ref, GATE_OFF + D_MODEL, D_MODEL))
    qs = [_proj(u, w_ref, b_ref, _q_col(g), ATT_WIDTH) * (HEAD_DIM ** -0.5) for g in range(N_GROUPS)]
    return glu, gate_c, gate_a, qs


def _residue_rows(t, scr_ref, slot, dil):
    if dil == 1:
        return [t]
    rows = t.shape[0]
    scr_ref[slot] = t
    return [scr_ref[slot, pl.ds(r, rows // dil, stride=dil), :] for r in range(dil)]


def _inproj_a_kernel(x_ref, mod_ref, w_ref, b_ref, glu_ref, gc_ref, ga_ref, q0_ref, q1_ref, q2_ref,
                     scr_ref):
    u = _modulated(x_ref, mod_ref, 0, 1)
    glu, gate_c, gate_a, qs = _glu_gates_q(u, w_ref, b_ref)
    glu_ref[0] = glu
    gc_ref[0] = gate_c.astype(gc_ref.dtype)
    ga_ref[0] = gate_a.astype(ga_ref.dtype)
    for g, (q, q_ref) in enumerate(zip(qs, (q0_ref, q1_ref, q2_ref))):
        dil = DILATIONS[g]
        for p in range(ATT_WIDTH // LANES):
            parts = _residue_rows(q[:, p * LANES:(p + 1) * LANES], scr_ref, p, dil)
            for r, t in enumerate(parts):
                lane = lax.broadcasted_iota(jnp.int32, t.shape, 1)
                q_ref[0, r, :, (2 * p) * LANES:(2 * p + 1) * LANES] = jnp.where(lane < HEAD_DIM, t, 0.0).astype(BF16)
                q_ref[0, r, :, (2 * p + 1) * LANES:(2 * p + 2) * LANES] = jnp.where(lane >= HEAD_DIM, t, 0.0).astype(BF16)


def _residue_spec(tm, dil, width):
    return pl.BlockSpec((1, dil, tm // dil, width), lambda b, i: (b, 0, i, 0))


def _inproj_a(x, mod, w_in, b_in, layer, tm):
    bsz, s, d = x.shape
    nt = s // tm
    tile = lambda width: pl.BlockSpec((1, tm, width), lambda b, i: (b, i, 0))
    out_shape = ([jax.ShapeDtypeStruct((bsz, s, CONV_CH), F32),
                  jax.ShapeDtypeStruct((bsz, s, d), GATE_DTYPE),
                  jax.ShapeDtypeStruct((bsz, s, d), GATE_DTYPE)]
                 + [jax.ShapeDtypeStruct((bsz, dil, s // dil, 2 * ATT_WIDTH), BF16) for dil in DILATIONS])
    return pl.pallas_call(
        _inproj_a_kernel,
        grid=(bsz, nt),
        in_specs=[tile(d),
                  pl.BlockSpec((1, 1, MOD_WIDTH), lambda b, i: (b, 0, 0)),
                  _layer_spec(w_in, layer), _layer_spec(b_in, layer)],
        out_specs=[tile(CONV_CH), tile(d), tile(d)]
                  + [_residue_spec(tm, dil, 2 * ATT_WIDTH) for dil in DILATIONS],
        out_shape=out_shape,
        scratch_shapes=[pltpu.VMEM((ATT_WIDTH // LANES, tm, LANES), F32)],
        compiler_params=_params(2),
        name="inproj_a",
    )(x, mod, w_in, b_in)


def _conv_ln_silu(win_ref, r0, rb, wdw_ref, bdw_ref, cg_ref, cb_ref, out_ref):
    off = CONV_HALO - (CONV_WIDTH - 1)
    accs = []
    for c in range(CONV_CH // LANES):
        cols = slice(c * LANES, (c + 1) * LANES)
        acc = jnp.zeros((rb, LANES), F32) + bdw_ref[0, :, cols]
        for j in range(CONV_WIDTH):
            acc = acc + win_ref[c, pl.ds(r0 + off + j, rb, stride=1), :] * wdw_ref[0, j:j + 1, cols]
        accs.append(acc)
    y = _silu(_layer_norm(jnp.concatenate(accs, axis=1), cg_ref[0], cb_ref[0]))
    out_ref[0, r0:r0 + rb, :] = y.astype(out_ref.dtype)


def _inproj_b_kernel(*refs, tm, seq, aliased):
    (x_ref, mod_ref, w_ref, b_ref, glu_ref, halo_ref,
     wdw_ref, bdw_ref, cg_ref, cb_ref) = refs[:10]
    n_in = 10 + (N_GROUPS if aliased else 0)
    kv_refs = refs[n_in:n_in + N_GROUPS]
    nat_refs = refs[n_in + N_GROUPS:n_in + 2 * N_GROUPS]
    ycv_ref, scr_ref, win_ref, keep_ref = refs[n_in + 2 * N_GROUPS:]
    ti = pl.program_id(1)
    halo = halo_ref[0]
    halo = jnp.where(ti > 0, halo, jnp.zeros_like(halo))
    glu = glu_ref[0]
    for c in range(CONV_CH // LANES):
        win_ref[c, 0:CONV_HALO, :] = halo[:, c * LANES:(c + 1) * LANES]
        win_ref[c, CONV_HALO:CONV_HALO + tm, :] = glu[:, c * LANES:(c + 1) * LANES]
    u = _modulated(x_ref, mod_ref, 0, 1)
    wins = [min(w, seq) for w in WINDOWS]
    every_tile = [w >= seq for w in wins]

    rb = 64
    for r0 in range(0, tm, rb):
        _conv_ln_silu(win_ref, r0, rb, wdw_ref, bdw_ref, cg_ref, cb_ref, ycv_ref)
    for g, kv_ref in enumerate(kv_refs):
        for half, c0 in enumerate((_k_col(g), _v_col(g))):
            t = _proj(u, w_ref, b_ref, c0, ATT_WIDTH)
            for p in range(ATT_WIDTH // LANES):
                parts = _residue_rows(t[:, p * LANES:(p + 1) * LANES], scr_ref, half * 4 + p, DILATIONS[g])
                for r, part in enumerate(parts):
                    cols = slice(half * ATT_WIDTH + p * LANES, half * ATT_WIDTH + (p + 1) * LANES)
                    kv_ref[0, r, :, cols] = part.astype(BF16)
            if every_tile[g]:
                nat_refs[g][0, half * ATT_WIDTH:(half + 1) * ATT_WIDTH, :] = t.T
            else:
                keep_ref[2 * g + half] = t
    for g, n_ref in enumerate(nat_refs):
        if every_tile[g]:
            continue
        win = wins[g]
        rows = min(win, tm)
        cond = (ti >= (seq - win) // tm) if win >= tm else (ti == seq // tm - 1)

        @pl.when(cond)
        def _(g=g, n_ref=n_ref, rows=rows):
            for half in range(2):
                n_ref[0, half * ATT_WIDTH:(half + 1) * ATT_WIDTH, :] = keep_ref[2 * g + half, tm - rows:tm, :].T


def _inproj_b(x, mod, w_in, b_in, glu, conv_w, layer, depth, prev_nat, tm):
    bsz, s, d = x.shape
    nt = s // tm
    assert tm % CONV_HALO == 0
    tile = lambda width: pl.BlockSpec((1, tm, width), lambda b, i: (b, i, 0))
    halo_spec = pl.BlockSpec(
        (1, CONV_HALO, CONV_CH), lambda b, i: (b, jnp.maximum(i * (tm // CONV_HALO) - 1, 0), 0))
    nat_specs, nat_shapes = [], []
    for g in range(N_GROUPS):
        win = min(WINDOWS[g], s)
        assert win % tm == 0 or tm % win == 0
        nat_shapes.append(jax.ShapeDtypeStruct((depth * bsz, KV_WIDTH, win), F32))
        if win >= tm:
            first = (s - win) // tm
            nat_specs.append(pl.BlockSpec(
                (1, KV_WIDTH, tm),
                lambda b, i, first=first: (layer * bsz + b, 0, jnp.maximum(i - first, 0))))
        else:
            nat_specs.append(pl.BlockSpec((1, KV_WIDTH, win), lambda b, i: (layer * bsz + b, 0, 0)))
    args = [x, mod, w_in, b_in, glu, glu, *conv_w]
    in_specs = [tile(d), pl.BlockSpec((1, 1, MOD_WIDTH), lambda b, i: (b, 0, 0)),
                _layer_spec(w_in, layer), _layer_spec(b_in, layer),
                tile(CONV_CH), halo_spec] + [_layer_spec(w, layer) for w in conv_w]
    aliases = {}
    if prev_nat is not None:
        for g in range(N_GROUPS):
            in_specs.append(pl.BlockSpec(memory_space=pl.ANY))
            aliases[len(args)] = N_GROUPS + g
            args.append(prev_nat[g])
    outs = pl.pallas_call(
        functools.partial(_inproj_b_kernel, tm=tm, seq=s, aliased=prev_nat is not None),
        grid=(bsz, nt),
        in_specs=in_specs,
        out_specs=[_residue_spec(tm, dil, KV_WIDTH) for dil in DILATIONS] + nat_specs + [tile(CONV_CH)],
        out_shape=[jax.ShapeDtypeStruct((bsz, dil, s // dil, KV_WIDTH), BF16) for dil in DILATIONS]
                  + nat_shapes + [jax.ShapeDtypeStruct((bsz, s, CONV_CH), BF16)],
        scratch_shapes=[pltpu.VMEM((KV_WIDTH // LANES, tm, LANES), F32),
                        pltpu.VMEM((CONV_CH // LANES, CONV_HALO + tm, LANES), F32),
                        pltpu.VMEM((2 * N_GROUPS, tm, ATT_WIDTH), F32)],
        input_output_aliases=aliases,
        compiler_params=_params(2),
        name="inproj_b",
    )(*args)
    return outs[:N_GROUPS], outs[N_GROUPS:2 * N_GROUPS], outs[2 * N_GROUPS]


def _inproj_s_kernel(x_ref, mod_ref, xq_ref, modq_ref, w_ref, b_ref,
                     glu_ref, gc_ref, ga_ref, q_ref, kvt_ref):
    u = _modulated(x_ref, mod_ref, 0, 1)
    glu, gate_c, gate_a, qs = _glu_gates_q(u, w_ref, b_ref)
    glu_ref[0] = glu
    gc_ref[0] = gate_c.astype(gc_ref.dtype)
    ga_ref[0] = gate_a.astype(ga_ref.dtype)
    for g, q in enumerate(qs):
        q_ref[0, :, g * ATT_WIDTH:(g + 1) * ATT_WIDTH] = q
    uq = _modulated(xq_ref, modq_ref, 0, 1)
    for g in range(N_GROUPS):
        for half, c0 in enumerate((_k_col(g), _v_col(g))):
            r0 = g * KV_WIDTH + half * ATT_WIDTH
            kvt_ref[r0:r0 + ATT_WIDTH, :] = _proj(uq, w_ref, b_ref, c0, ATT_WIDTH).T


def _inproj_s(x, mod, xq, modq, w_in, b_in, layer):
    _, m, d = x.shape
    full = lambda *shape: _const_spec(shape)
    whole = lambda *shape: pl.BlockSpec(shape, lambda i: (0,) * len(shape))
    return pl.pallas_call(
        _inproj_s_kernel,
        in_specs=[full(1, m, d), full(1, m, MOD_WIDTH), full(1, m, d), full(1, m, MOD_WIDTH),
                  _layer_spec(w_in, layer), _layer_spec(b_in, layer)],
        out_specs=[whole(1, m, CONV_CH), whole(1, m, d), whole(1, m, d),
                   whole(1, m, N_GROUPS * ATT_WIDTH), whole(N_GROUPS * KV_WIDTH, m)],
        out_shape=[jax.ShapeDtypeStruct((1, m, CONV_CH), F32),
                   jax.ShapeDtypeStruct((1, m, d), GATE_DTYPE),
                   jax.ShapeDtypeStruct((1, m, d), GATE_DTYPE),
                   jax.ShapeDtypeStruct((1, m, N_GROUPS * ATT_WIDTH), F32),
                   jax.ShapeDtypeStruct((N_GROUPS * KV_WIDTH, m), F32)],
        grid=(1,),
        compiler_params=_params(1),
        name="inproj_s",
    )(x, mod, xq, modq, w_in, b_in)


def _alibi_slopes():
    return (2.0 ** (-8.0 * (np.arange(N_SLOTS) + 1) / N_SLOTS)).astype(np.float32)


def _prompt_bias(dil, key_blocks):
    qi = np.arange(BLOCK)[:, None]
    kj = np.arange(key_blocks * BLOCK)[None, :]
    tables = []
    for lead in (0, key_blocks - 1):
        rel = lead * BLOCK + qi - kj
        valid = (rel >= 0) & (rel <= SPAN)
        bias = -_alibi_slopes()[:, None, None] * (rel * dil).astype(np.float32)[None]
        tables.append(np.where(valid[None], bias, np.float32(NEG)))
    return np.stack(tables).astype(np.float32)


def _attn_kernel(q_ref, kv_ref, bias_ref, o_ref, lse_ref, s_scr, m_scr, *, nb, n_units, dil):
    kw = bias_ref.shape[-1]
    lane = lax.broadcasted_iota(jnp.int32, (BLOCK, LANES), 1)
    pick = lambda a, b: jnp.where(lane < HEAD_DIM, a, b)

    def place(u):
        if isinstance(u, int):
            rr, c = divmod(u, nb)
            r0 = c * BLOCK
            k0 = max(r0 - (kw - BLOCK), 0)
            return rr, r0, k0, (0 if c == 0 else 1)
        rr = u // nb
        c = u % nb
        r0 = pl.multiple_of(c * BLOCK, BLOCK)
        k0 = pl.multiple_of(jnp.maximum(r0 - (kw - BLOCK), 0), BLOCK)
        return rr, r0, k0, jnp.minimum(c, 1)

    ones = jnp.ones((kw, LANES), BF16)

    def scores(u, slot):
        rr, r0, k0, tbl = place(u)
        for p in range(ATT_WIDTH // LANES):
            kp = kv_ref[0, rr, pl.ds(k0, kw), p * LANES:(p + 1) * LANES]
            q2 = q_ref[0, rr, pl.ds(r0, BLOCK), 2 * p * LANES:(2 * p + 2) * LANES]
            q2 = jnp.concatenate([q2[:, :LANES], q2[:, LANES:]], axis=0)
            bias2 = jnp.concatenate([bias_ref[tbl, 2 * p], bias_ref[tbl, 2 * p + 1]], axis=0)
            s = _dot_nt(q2, kp) + bias2
            m = jnp.max(s, axis=-1, keepdims=True)
            s_scr[slot, p] = jnp.exp(s - m).astype(BF16)
            m_scr[slot, p] = m

    def finish(u, slot):
        rr, r0, k0, _ = place(u)
        for p in range(ATT_WIDTH // LANES):
            vp = kv_ref[0, rr, pl.ds(k0, kw), ATT_WIDTH + p * LANES:ATT_WIDTH + (p + 1) * LANES]
            m = m_scr[slot, p]
            pv = _dot(s_scr[slot, p], jnp.concatenate([vp, ones], axis=1))
            l2 = pick(pv[:BLOCK, LANES:], pv[BLOCK:, LANES:])
            if dil == 1:
                rows = pl.ds(r0, BLOCK)
            else:
                rows = pl.ds(r0 * dil + rr, BLOCK, stride=dil)
            o_ref[0, p, rows, :] = pick(pv[:BLOCK, :LANES], pv[BLOCK:, :LANES]) / l2
            lse_ref[0, p, rows, :] = pick(m[:BLOCK], m[BLOCK:]) + jnp.log(l2)

    scores(0, 0)

    def body(u, carry):
        slot = u % 2
        finish(u, slot)
        scores(u + 1, 1 - slot)
        return carry
    lax.fori_loop(0, n_units - 1, body, 0)
    finish(n_units - 1, (n_units - 1) % 2)


def _prompt_attention(qx, kvb, dil):
    bsz, _, n, _ = qx.shape
    assert n % BLOCK == 0
    nb = n // BLOCK
    bias = jnp.asarray(_prompt_bias(dil, min(nb, 2)))
    sub = lambda width: pl.BlockSpec((1, dil, n, width), lambda b: (b, 0, 0, 0))
    slabs = pl.BlockSpec((1, ATT_WIDTH // LANES, dil * n, LANES), lambda b: (b, 0, 0, 0))
    return pl.pallas_call(
        functools.partial(_attn_kernel, nb=nb, n_units=dil * nb, dil=dil),
        grid=(bsz,),
        in_specs=[sub(2 * ATT_WIDTH), sub(KV_WIDTH), _const_spec(bias.shape)],
        out_specs=[slabs, slabs],
        out_shape=[jax.ShapeDtypeStruct((bsz, ATT_WIDTH // LANES, dil * n, LANES), F32)] * 2,
        scratch_shapes=[pltpu.VMEM((2, N_SLOTS // 2, 2 * BLOCK, bias.shape[-1]), BF16),
                        pltpu.VMEM((2, N_SLOTS // 2, 2 * BLOCK, 1), F32)],
        compiler_params=_params(1),
        name=f"attn_d{dil}",
    )(qx, kvb, bias)


def _combine_groups(os_, lses):
    m = functools.reduce(jnp.maximum, lses)
    ws = [jnp.exp(l - m) for l in lses]
    den = functools.reduce(lambda a, b: a + b, ws)
    num = functools.reduce(lambda a, b: a + b, [w * o for w, o in zip(ws, os_)])
    return num / den


def _merge_to_ln1(x, m, ybuf, o_att, gate_c, gate_a,
                  wpc_ref, bpc_ref, wpa_ref, bpa_ref, wout_ref, bout_ref, g1_ref, b1_ref, alpha):
    y_conv = _dot(ybuf, wpc_ref[0]) + bpc_ref[0]
    y_att = _dot(o_att.astype(BF16), wpa_ref[0]) + bpa_ref[0]
    y = (gate_c * y_conv + gate_a * y_att).astype(BF16)
    y = _dot(y, wout_ref[0]) + bout_ref[0]
    return _layer_norm(alpha * x + (1.0 + _mod_slice(m, 2)) * y, g1_ref[0], b1_ref[0])


def _ffn_to_ln2(x1, m, wg_ref, wu_ref, wd_ref, g2_ref, b2_ref, alpha, chunk):
    u2 = (x1 * (1.0 + _mod_slice(m, 4)) + _mod_slice(m, 3)).astype(BF16)
    d_ff = wg_ref.shape[-1]
    h = None
    for c0 in range(0, d_ff, chunk):
        c1 = min(c0 + chunk, d_ff)
        a = (_silu(_dot(u2, wg_ref[0, :, c0:c1])) * _dot(u2, wu_ref[0, :, c0:c1])).astype(BF16)
        part = _dot(a, wd_ref[0, c0:c1, :])
        h = part if h is None else h + part
    return _layer_norm(alpha * x1 + (1.0 + _mod_slice(m, 5)) * h, g2_ref[0], b2_ref[0])


def _post_a_kernel(x_ref, mod_ref, ycv_ref, o0_ref, o1_ref, o2_ref, l0_ref, l1_ref, l2_ref,
                   gc_ref, ga_ref, wpc_ref, bpc_ref, wpa_ref, bpa_ref, wout_ref, bout_ref,
                   g1_ref, b1_ref, out_ref, *, alpha):
    o_refs = (o0_ref, o1_ref, o2_ref)
    l_refs = (l0_ref, l1_ref, l2_ref)
    chunks = []
    for p in range(ATT_WIDTH // LANES):
        os_ = [o_refs[g][0, p] for g in range(N_GROUPS)]
        lses = [l_refs[g][0, p] for g in range(N_GROUPS)]
        chunks.append(_combine_groups(os_, lses).astype(BF16))
    o_att = jnp.concatenate(chunks, axis=1)
    out_ref[0] = _merge_to_ln1(x_ref[0], mod_ref[0], ycv_ref[0], o_att,
                               gc_ref[0].astype(F32), ga_ref[0].astype(F32),
                               wpc_ref, bpc_ref, wpa_ref, bpa_ref, wout_ref, bout_ref,
                               g1_ref, b1_ref, alpha)


def _post_a(x, mod, ycv, os_, lses, gate_c, gate_a, wts, layer, tm, alpha):
    bsz, s, d = x.shape
    nt = s // tm
    tile = lambda width: pl.BlockSpec((1, tm, width), lambda b, i: (b, i, 0))
    slabs = pl.BlockSpec((1, ATT_WIDTH // LANES, tm, LANES), lambda b, i: (b, 0, i, 0))
    return pl.pallas_call(
        functools.partial(_post_a_kernel, alpha=alpha),
        grid=(bsz, nt),
        in_specs=([tile(d), pl.BlockSpec((1, 1, MOD_WIDTH), lambda b, i: (b, 0, 0)), tile(CONV_CH)]
                  + [slabs] * (2 * N_GROUPS) + [tile(d), tile(d)]
                  + [_layer_spec(w, layer) for w in wts]),
        out_specs=tile(d),
        out_shape=jax.ShapeDtypeStruct((bsz, s, d), F32),
        compiler_params=_params(2),
        name="post_a",
    )(x, mod, ycv, *os_, *lses, gate_c, gate_a, *wts)


def _post_b_kernel(*refs, alpha, chunk, t_new, nsq, nblk):
    x_ref, mod_ref, wg_ref, wu_ref, wd_ref, g2_ref, b2_ref = refs[:7]
    q_refs, c_refs, n_refs = refs[7:10], refs[10:13], refs[13:16]
    bc_refs, bn_refs = refs[16:19], refs[19:22]
    out_ref, o_ref = refs[-5:-3]
    oc_refs = refs[-3:]
    out_ref[0] = _ffn_to_ln2(x_ref[0], mod_ref[0], wg_ref, wu_ref, wd_ref, g2_ref, b2_ref, alpha, chunk)
    step = pl.program_id(0) * pl.num_programs(1) + pl.program_id(1)
    first_seq = (step % nblk) * nsq
    for i in range(nsq):
        _sample_unit(q_refs, c_refs, n_refs, bc_refs, bn_refs, o_ref, oc_refs, i, first_seq + i, t_new)


def _post_b(x1, mod, wts, layer, tm, alpha, q8, caches_n, kvt_new, prev_outs, nseq, t_new):
    bsz, s, d = x1.shape
    nt = s // tm
    hsteps = ATT_WIDTH // LANES
    steps = bsz * nt
    assert (hsteps * nseq) % steps == 0 and nseq * t_new == LANES
    nsq = hsteps * nseq // steps
    nblk = nseq // nsq
    side = lambda f: (lambda b, i: f(*divmod(b * nt + i, nblk)))
    tables = [_sample_bias(g, caches_n[g].shape[-1], t_new) for g in range(N_GROUPS)]
    bc = [jnp.asarray(tc.reshape(hsteps, 2, Q_ROWS, -1)) for tc, _ in tables]
    bn = [jnp.asarray(tn.reshape(hsteps, 2, Q_ROWS, LANES)) for _, tn in tables]
    kvt4 = kvt_new.reshape(N_GROUPS, 2, ATT_WIDTH, nseq * t_new)
    tile = pl.BlockSpec((1, tm, d), lambda b, i: (b, i, 0))
    args = [x1, mod, *wts]
    in_specs = ([tile, pl.BlockSpec((1, 1, MOD_WIDTH), lambda b, i: (b, 0, 0))]
                + [_layer_spec(w, layer) for w in wts])
    for g in range(N_GROUPS):
        in_specs.append(pl.BlockSpec((nsq, Q_ROWS, LANES), side(lambda hc, blk, g=g: (blk, 0, g * hsteps + hc))))
        args.append(q8)
    cache_specs = [pl.BlockSpec((nsq, 2, LANES, c.shape[-1]),
                                side(lambda hc, blk: (layer * nblk + blk, 0, hc, 0))) for c in caches_n]
    in_specs += cache_specs
    args += list(caches_n)
    for g in range(N_GROUPS):
        in_specs.append(pl.BlockSpec((1, 2, LANES, LANES), side(lambda hc, blk, g=g: (g, 0, hc, 0))))
        args.append(kvt4)
    for tbl in bc + bn:
        in_specs.append(pl.BlockSpec((1,) + tbl.shape[1:], side(lambda hc, blk: (hc, 0, 0, 0))))
        args.append(tbl)
    aliases = {}
    if prev_outs is not None:
        for g in range(N_GROUPS):
            in_specs.append(pl.BlockSpec(memory_space=pl.ANY))
            aliases[len(args)] = 2 + g
            args.append(prev_outs[g])
    outs = pl.pallas_call(
        functools.partial(_post_b_kernel, alpha=alpha, chunk=1024, t_new=t_new, nsq=nsq, nblk=nblk),
        grid=(bsz, nt),
        in_specs=in_specs,
        out_specs=[tile, pl.BlockSpec((nsq, Q_ROWS, LANES), side(lambda hc, blk: (blk, 0, hc)))] + cache_specs,
        out_shape=[jax.ShapeDtypeStruct((bsz, s, d), F32),
                   jax.ShapeDtypeStruct((nseq, Q_ROWS, ATT_WIDTH), F32)]
                  + [jax.ShapeDtypeStruct(c.shape, F32) for c in caches_n],
        input_output_aliases=aliases,
        compiler_params=_params(2),
        name="post_b",
    )(*args)
    return outs[0], outs[1], outs[2:]


def _post_s_kernel(x_ref, mod_ref, state_ref, glu_ref, o_ref, gc_ref, ga_ref,
                   wdw_ref, bdw_ref, cg_ref, cb_ref,
                   wpc_ref, bpc_ref, wpa_ref, bpa_ref, wout_ref, bout_ref, g1_ref, b1_ref,
                   wg_ref, wu_ref, wd_ref, g2_ref, b2_ref,
                   out_ref, nconv_ref, ypre_ref, *, nseq, t_new, alpha, chunk):
    ctx = CONV_WIDTH - 1

    def slab(i):
        if i < ctx:
            return state_ref[0, i]
        return glu_ref[0, (i - ctx) * nseq:(i - ctx + 1) * nseq, :]

    for t in range(t_new):
        acc = jnp.zeros((nseq, CONV_CH), F32) + bdw_ref[0]
        for j in range(CONV_WIDTH):
            acc = acc + slab(t + j) * wdw_ref[0, j:j + 1, :]
        ypre_ref[t * nseq:(t + 1) * nseq, :] = acc
    for i in range(ctx):
        nconv_ref[0, i] = slab(i + t_new)
    ybuf = _silu(_layer_norm(ypre_ref[...], cg_ref[0], cb_ref[0])).astype(BF16)
    m = mod_ref[0]
    x1 = _merge_to_ln1(x_ref[0], m, ybuf, o_ref[0], gc_ref[0].astype(F32), ga_ref[0].astype(F32),
                       wpc_ref, bpc_ref, wpa_ref, bpa_ref, wout_ref, bout_ref, g1_ref, b1_ref, alpha)
    out_ref[0] = _ffn_to_ln2(x1, m, wg_ref, wu_ref, wd_ref, g2_ref, b2_ref, alpha, chunk)


def _post_s(x, mod, state_n, layer, glu, o_att, gate_c, gate_a, wts, nseq, t_new, alpha):
    _, m, d = x.shape
    ctx = CONV_WIDTH - 1
    full = lambda *shape: _const_spec(shape)
    whole = lambda *shape: pl.BlockSpec(shape, lambda i: (0,) * len(shape))
    return pl.pallas_call(
        functools.partial(_post_s_kernel, nseq=nseq, t_new=t_new, alpha=alpha, chunk=1024),
        in_specs=[full(1, m, d), full(1, m, MOD_WIDTH), _layer_spec(state_n, layer), full(1, m, CONV_CH),
                  full(1, m, ATT_WIDTH), full(1, m, d), full(1, m, d)]
                 + [_layer_spec(w, layer) for w in wts],
        out_specs=[whole(1, m, d), whole(1, ctx, nseq, CONV_CH)],
        out_shape=[jax.ShapeDtypeStruct((1, m, d), F32),
                   jax.ShapeDtypeStruct((1, ctx, nseq, CONV_CH), F32)],
        scratch_shapes=[pltpu.VMEM((m, CONV_CH), F32)],
        grid=(1,),
        compiler_params=_params(1),
        name="post_s",
    )(x, mod, state_n, glu, o_att, gate_c, gate_a, *wts)


def _sample_bias(g, length, t_new):
    dil = DILATIONS[g]
    slopes = _alibi_slopes()[:, None, None]
    t = np.arange(Q_ROWS)[:, None]
    p = np.arange(length)[None, :]
    dist = length + t - p
    valid = (dist % dil == 0) & (dist // dil >= 1) & (dist // dil <= SPAN) & (t < t_new)
    cache = np.where(valid[None], -slopes * dist.astype(np.float32)[None], np.float32(NEG))
    tp = np.arange(LANES)[None, :] - (LANES - t_new)
    dist_n = t - tp
    valid_n = (tp >= 0) & (dist_n >= 0) & (dist_n % dil == 0) & (dist_n // dil <= SPAN) & (t < t_new)
    new = np.where(valid_n[None], -slopes * dist_n.astype(np.float32)[None], np.float32(NEG))
    return cache.astype(np.float32), new.astype(np.float32)


def _sample_unit(q_refs, c_refs, n_refs, bc_refs, bn_refs, o_ref, oc_refs, i, seq, t_new):
    shift = (LANES - t_new) - t_new * seq
    lane = lax.broadcasted_iota(jnp.int32, (LANES, LANES), 1)
    heads_per_step = LANES // HEAD_DIM
    per_head = [[] for _ in range(heads_per_step)]
    for g in range(N_GROUPS):
        c_ref = c_refs[g]
        length = c_ref.shape[-1]
        new = [pltpu.roll(n_refs[g][0, kv], shift, axis=1) for kv in range(2)]
        for e in range(heads_per_step):
            rows = slice(e * HEAD_DIM, (e + 1) * HEAD_DIM)
            qh = q_refs[g][i, :, rows].astype(BF16)
            s_c = _dot(qh, c_ref[i, 0, rows, :].astype(BF16)) + bc_refs[g][0, e]
            s_n = _dot(qh, new[0][rows, :].astype(BF16)) + bn_refs[g][0, e]
            m = jnp.maximum(jnp.max(s_c, axis=-1, keepdims=True), jnp.max(s_n, axis=-1, keepdims=True))
            p_c = jnp.exp(s_c - m)
            p_n = jnp.exp(s_n - m)
            l = jnp.sum(p_c, axis=-1, keepdims=True) + jnp.sum(p_n, axis=-1, keepdims=True)
            o = (_dot_nt(p_c.astype(BF16), c_ref[i, 1, rows, :].astype(BF16))
                 + _dot_nt(p_n.astype(BF16), new[1][rows, :].astype(BF16)))
            per_head[e].append((o / l, m + jnp.log(l)))
        for kv in range(2):
            rolled = pltpu.roll(c_ref[i, kv], length - t_new, axis=1)
            if length > LANES:
                oc_refs[g][i, kv, :, 0:length - LANES] = rolled[:, 0:length - LANES]
            oc_refs[g][i, kv, :, length - LANES:length] = jnp.where(
                lane >= LANES - t_new, new[kv], rolled[:, length - LANES:length])
    for e in range(heads_per_step):
        os_, lses = zip(*per_head[e])
        o_ref[i, :, e * HEAD_DIM:(e + 1) * HEAD_DIM] = _combine_groups(list(os_), list(lses))


def _native_view(cache):
    depth, nseq, length = cache.shape[:3]
    return cache.transpose(0, 1, 3, 4, 5, 2).reshape(depth * nseq, 2, ATT_WIDTH, length)


def _from_native(x, depth, nseq):
    length = x.shape[-1]
    return x.reshape(depth, nseq, 2, N_SLOTS, HEAD_DIM, length).transpose(0, 1, 5, 2, 3, 4)


def kernel(x_prompt, x_sample, c_prompt, c_sample, cache_kv_g0, cache_kv_g1, cache_kv_g2, state_conv, w_ada, b_ada, w_in, b_in, w_dw, b_dw, conv_ln_g, conv_ln_b, w_pc, b_pc, w_pa, b_pa, w_out, b_out, ln1_g, ln1_b, w_gate, w_up, w_down, ln2_g, ln2_b):
    depth = w_in.shape[0]
    bsz, seq, d = x_prompt.shape
    nseq, t_new, _ = x_sample.shape
    alpha = (2 * depth) ** 0.25
    tm = 512
    tm_a = 1024
    tm_ffn = 256
    m_s = nseq * t_new

    mod_all = _adaln(jnp.concatenate([c_prompt, c_sample], axis=0), w_ada, b_ada)
    caches_n = [_native_view(c) for c in (cache_kv_g0, cache_kv_g1, cache_kv_g2)]
    state_n = state_conv.transpose(0, 2, 1, 3)

    vec = lambda v: v.reshape(depth, 1, -1)
    w_in_bf = w_in.astype(BF16)
    b_in3 = vec(b_in)
    conv_w = [w_dw, vec(b_dw), vec(conv_ln_g), vec(conv_ln_b)]
    merge_w = [w_pc.astype(BF16), vec(b_pc), w_pa.astype(BF16), vec(b_pa),
               w_out.astype(BF16), vec(b_out), vec(ln1_g), vec(ln1_b)]
    ffn_w = [w_gate.astype(BF16), w_up.astype(BF16), w_down.astype(BF16), vec(ln2_g), vec(ln2_b)]

    xp = x_prompt
    xs = x_sample.transpose(1, 0, 2).reshape(1, m_s, d)
    conv_prompt, conv_sample = [], []
    cache_outs = None
    kv_nat = None
    for l in range(depth):
        mod_p = mod_all[l, :bsz].reshape(bsz, 1, MOD_WIDTH)
        mod_s = jnp.tile(mod_all[l, bsz:], (t_new, 1)).reshape(1, m_s, MOD_WIDTH)
        mod_sq = jnp.repeat(mod_all[l, bsz:], t_new, axis=0).reshape(1, m_s, MOD_WIDTH)

        xs_q = xs.reshape(t_new, nseq, d).transpose(1, 0, 2).reshape(1, m_s, d)
        glu_s, gc_s, ga_s, q_s, kvt_new = _inproj_s(xs, mod_s, xs_q, mod_sq, w_in_bf, b_in3, l)
        q8 = jnp.pad(q_s.reshape(t_new, nseq, N_GROUPS * ATT_WIDTH).transpose(1, 0, 2),
                     ((0, 0), (0, Q_ROWS - t_new), (0, 0)))

        glu, gate_c, gate_a, *qx = _inproj_a(xp, mod_p, w_in_bf, b_in3, l, tm_a)
        kvb, kv_nat, ycv = _inproj_b(xp, mod_p, w_in_bf, b_in3, glu, conv_w, l, depth, kv_nat, tm)
        conv_prompt.append(glu[:, seq - (CONV_WIDTH - 1):, :])
        os_, lses = [], []
        for g in range(N_GROUPS):
            o, lse = _prompt_attention(qx[g], kvb[g], DILATIONS[g])
            os_.append(o)
            lses.append(lse)
        x1 = _post_a(xp, mod_p, ycv, os_, lses, gate_c, gate_a, merge_w, l, tm, alpha)
        xp, o8, cache_outs = _post_b(x1, mod_p, ffn_w, l, tm_ffn, alpha,
                                     q8, caches_n, kvt_new, cache_outs, nseq, t_new)

        o_s = o8[:, :t_new].transpose(1, 0, 2).reshape(1, m_s, ATT_WIDTH)
        xs, nconv = _post_s(xs, mod_s, state_n, l, glu_s, o_s, gc_s, ga_s, conv_w + merge_w + ffn_w,
                            nseq, t_new, alpha)
        conv_sample.append(nconv[0])

    kv_p = [_from_native(kv_nat[g].reshape(depth * bsz, 2, ATT_WIDTH, -1), depth, bsz)
            for g in range(N_GROUPS)]
    kv_s = [_from_native(cache_outs[g], depth, nseq) for g in range(N_GROUPS)]
    return (xp, xs.reshape(t_new, nseq, d).transpose(1, 0, 2),
            kv_p[0], kv_p[1], kv_p[2], jnp.stack(conv_prompt),
            kv_s[0], kv_s[1], kv_s[2], jnp.stack(conv_sample).transpose(0, 2, 1, 3))
```

```python
import functools

import numpy as np
import jax
import jax.numpy as jnp
from jax import lax
from jax.experimental import pallas as pl
from jax.experimental.pallas import tpu as pltpu

D_MODEL = 1024
CONV_CH = 512
CONV_WIDTH = 31
N_SLOTS = 8
HEAD_DIM = 64
ATT_WIDTH = N_SLOTS * HEAD_DIM
WINDOWS = (128, 512, 2048)
DILATIONS = (1, 4, 16)
N_GROUPS = 3
SPAN = 128
BLOCK = 128
LN_EPS = 1e-5
QKV_OFF = 2 * CONV_CH
GATE_OFF = QKV_OFF + 3 * N_GROUPS * ATT_WIDTH
IN_WIDTH = GATE_OFF + 2 * D_MODEL
KV_WIDTH = 2 * ATT_WIDTH
MOD_WIDTH = 6 * D_MODEL
LANES = 128
NEG = -1e30
Q_ROWS = 8
CONV_HALO = 32
VMEM_LIMIT = 56 * 1024 * 1024

F32 = jnp.float32
BF16 = jnp.bfloat16
GATE_DTYPE = jnp.bfloat16

_NT = (((1,), (1,)), ((), ()))


def _q_col(g):
    return QKV_OFF + g * ATT_WIDTH


def _k_col(g):
    return QKV_OFF + (N_GROUPS + g) * ATT_WIDTH


def _v_col(g):
    return QKV_OFF + (2 * N_GROUPS + g) * ATT_WIDTH


def _dot(a, b):
    return jnp.dot(a, b, preferred_element_type=F32)


def _dot_nt(a, b):
    return lax.dot_general(a, b, _NT, preferred_element_type=F32)


def _sigmoid(x):
    return 1.0 / (1.0 + jnp.exp(-x))


def _silu(x):
    return x * _sigmoid(x)


def _layer_norm(x, g, b):
    mu = jnp.mean(x, axis=-1, keepdims=True)
    xc = x - mu
    var = jnp.mean(xc * xc, axis=-1, keepdims=True)
    return xc * lax.rsqrt(var + LN_EPS) * g + b


def _mod_slice(m, i):
    return m[:, i * D_MODEL:(i + 1) * D_MODEL]


def _const_spec(shape):
    nd = len(shape)
    return pl.BlockSpec(shape, lambda *_: (0,) * nd, pipeline_mode=pl.Buffered(1))


def _layer_spec(arr, layer):
    nd = arr.ndim
    return pl.BlockSpec((1,) + arr.shape[1:], lambda *_: (layer,) + (0,) * (nd - 1),
                        pipeline_mode=pl.Buffered(1))


def _params(n_axes):
    return pltpu.CompilerParams(dimension_semantics=("arbitrary",) * n_axes,
                                vmem_limit_bytes=VMEM_LIMIT)


def _adaln_kernel(c_ref, w_ref, b_ref, o_ref):
    c = c_ref[...]
    s = _silu(c).astype(BF16)
    o_ref[0] = _dot(s, w_ref[0].astype(BF16)) + b_ref[0]


def _adaln(c_all, w_ada, b_ada):
    depth, d, width = w_ada.shape
    rows = c_all.shape[0]
    tn = 1024
    return pl.pallas_call(
        _adaln_kernel,
        grid=(depth, width // tn),
        in_specs=[pl.BlockSpec((rows, d), lambda l, j: (0, 0)),
                  pl.BlockSpec((1, d, tn), lambda l, j: (l, 0, j)),
                  pl.BlockSpec((1, 1, tn), lambda l, j: (l, 0, j))],
        out_specs=pl.BlockSpec((1, rows, tn), lambda l, j: (l, 0, j)),
        out_shape=jax.ShapeDtypeStruct((depth, rows, width), F32),
        compiler_params=_params(2),
        name="adaln",
    )(c_all, w_ada, b_ada.reshape(depth, 1, width))


def _modulated(x_ref, mod_ref, shift_i, scale_i):
    m = mod_ref[0]
    x = x_ref[0]
    return (x * (1.0 + _mod_slice(m, scale_i)) + _mod_slice(m, shift_i)).astype(BF16)


def _proj(u, w_ref, b_ref, c0, width):
    return _dot(u, w_ref[0, :, c0:c0 + width]) + b_ref[0, :, c0:c0 + width]


def _glu_gates_q(u, w_ref, b_ref):
    glu = _proj(u, w_ref, b_ref, 0, CONV_CH) * _sigmoid(_proj(u, w_ref, b_ref, CONV_CH, CONV_CH))
    gate_c = _sigmoid(_proj(u, w_ref, b_ref, GATE_OFF, D_MODEL))
    gate_a = _sigmoid(_proj(u, w_ref, b_ref, GATE_OFF + D_MODEL, D_MODEL))
    qs = [_proj(u, w_ref, b_ref, _q_col(g), ATT_WIDTH) * (HEAD_DIM ** -0.5) for g in range(N_GROUPS)]
    return glu, gate_c, gate_a, qs


def _residue_rows(t, scr_ref, slot, dil):
    if dil == 1:
        return [t]
    rows = t.shape[0]
    scr_ref[slot] = t
    return [scr_ref[slot, pl.ds(r, rows // dil, stride=dil), :] for r in range(dil)]


def _inproj_a_kernel(x_ref, mod_ref, w_ref, b_ref, glu_ref, gc_ref, ga_ref, q0_ref, q1_ref, q2_ref,
                     scr_ref):
    u = _modulated(x_ref, mod_ref, 0, 1)
    glu, gate_c, gate_a, qs = _glu_gates_q(u, w_ref, b_ref)
    glu_ref[0] = glu
    gc_ref[0] = gate_c.astype(gc_ref.dtype)
    ga_ref[0] = gate_a.astype(ga_ref.dtype)
    for g, (q, q_ref) in enumerate(zip(qs, (q0_ref, q1_ref, q2_ref))):
        dil = DILATIONS[g]
        for p in range(ATT_WIDTH // LANES):
            parts = _residue_rows(q[:, p * LANES:(p + 1) * LANES], scr_ref, p, dil)
            for r, t in enumerate(parts):
                lane = lax.broadcasted_iota(jnp.int32, t.shape, 1)
                q_ref[0, r, :, (2 * p) * LANES:(2 * p + 1) * LANES] = jnp.where(lane < HEAD_DIM, t, 0.0).astype(BF16)
                q_ref[0, r, :, (2 * p + 1) * LANES:(2 * p + 2) * LANES] = jnp.where(lane >= HEAD_DIM, t, 0.0).astype(BF16)


def _residue_spec(tm, dil, width):
    return pl.BlockSpec((1, dil, tm // dil, width), lambda b, i: (b, 0, i, 0))


def _inproj_a(x, mod, w_in, b_in, layer, tm):
    bsz, s, d = x.shape
    nt = s // tm
    tile = lambda width: pl.BlockSpec((1, tm, width), lambda b, i: (b, i, 0))
    out_shape = ([jax.ShapeDtypeStruct((bsz, s, CONV_CH), F32),
                  jax.ShapeDtypeStruct((bsz, s, d), GATE_DTYPE),
                  jax.ShapeDtypeStruct((bsz, s, d), GATE_DTYPE)]
                 + [jax.ShapeDtypeStruct((bsz, dil, s // dil, 2 * ATT_WIDTH), BF16) for dil in DILATIONS])
    return pl.pallas_call(
        _inproj_a_kernel,
        grid=(bsz, nt),
        in_specs=[tile(d),
                  pl.BlockSpec((1, 1, MOD_WIDTH), lambda b, i: (b, 0, 0)),
                  _layer_spec(w_in, layer), _layer_spec(b_in, layer)],
        out_specs=[tile(CONV_CH), tile(d), tile(d)]
                  + [_residue_spec(tm, dil, 2 * ATT_WIDTH) for dil in DILATIONS],
        out_shape=out_shape,
        scratch_shapes=[pltpu.VMEM((ATT_WIDTH // LANES, tm, LANES), F32)],
        compiler_params=_params(2),
        name="inproj_a",
    )(x, mod, w_in, b_in)


def _conv_ln_silu(win_ref, r0, rb, wdw_ref, bdw_ref, cg_ref, cb_ref, out_ref):
    off = CONV_HALO - (CONV_WIDTH - 1)
    accs = []
    for c in range(CONV_CH // LANES):
        cols = slice(c * LANES, (c + 1) * LANES)
        acc = jnp.zeros((rb, LANES), F32) + bdw_ref[0, :, cols]
        for j in range(CONV_WIDTH):
            acc = acc + win_ref[c, pl.ds(r0 + off + j, rb, stride=1), :] * wdw_ref[0, j:j + 1, cols]
        accs.append(acc)
    y = _silu(_layer_norm(jnp.concatenate(accs, axis=1), cg_ref[0], cb_ref[0]))
    out_ref[0, r0:r0 + rb, :] = y.astype(out_ref.dtype)


def _inproj_b_kernel(*refs, tm, seq, aliased):
    (x_ref, mod_ref, w_ref, b_ref, glu_ref, halo_ref,
     wdw_ref, bdw_ref, cg_ref, cb_ref) = refs[:10]
    n_in = 10 + (N_GROUPS if aliased else 0)
    kv_refs = refs[n_in:n_in + N_GROUPS]
    nat_refs = refs[n_in + N_GROUPS:n_in + 2 * N_GROUPS]
    ycv_ref, scr_ref, win_ref, keep_ref = refs[n_in + 2 * N_GROUPS:]
    ti = pl.program_id(1)
    halo = halo_ref[0]
    halo = jnp.where(ti > 0, halo, jnp.zeros_like(halo))
    glu = glu_ref[0]
    for c in range(CONV_CH // LANES):
        win_ref[c, 0:CONV_HALO, :] = halo[:, c * LANES:(c + 1) * LANES]
        win_ref[c, CONV_HALO:CONV_HALO + tm, :] = glu[:, c * LANES:(c + 1) * LANES]
    u = _modulated(x_ref, mod_ref, 0, 1)
    wins = [min(w, seq) for w in WINDOWS]
    every_tile = [w >= seq for w in wins]

    rb = 64
    for r0 in range(0, tm, rb):
        _conv_ln_silu(win_ref, r0, rb, wdw_ref, bdw_ref, cg_ref, cb_ref, ycv_ref)
    for g, kv_ref in enumerate(kv_refs):
        for half, c0 in enumerate((_k_col(g), _v_col(g))):
            t = _proj(u, w_ref, b_ref, c0, ATT_WIDTH)
            for p in range(ATT_WIDTH // LANES):
                parts = _residue_rows(t[:, p * LANES:(p + 1) * LANES], scr_ref, half * 4 + p, DILATIONS[g])
                for r, part in enumerate(parts):
                    cols = slice(half * ATT_WIDTH + p * LANES, half * ATT_WIDTH + (p + 1) * LANES)
                    kv_ref[0, r, :, cols] = part.astype(BF16)
            if every_tile[g]:
                nat_refs[g][0, half * ATT_WIDTH:(half + 1) * ATT_WIDTH, :] = t.T
            else:
                keep_ref[2 * g + half] = t
    for g, n_ref in enumerate(nat_refs):
        if every_tile[g]:
            continue
        win = wins[g]
        rows = min(win, tm)
        cond = (ti >= (seq - win) // tm) if win >= tm else (ti == seq // tm - 1)

        @pl.when(cond)
        def _(g=g, n_ref=n_ref, rows=rows):
            for half in range(2):
                n_ref[0, half * ATT_WIDTH:(half + 1) * ATT_WIDTH, :] = keep_ref[2 * g + half, tm - rows:tm, :].T


def _inproj_b(x, mod, w_in, b_in, glu, conv_w, layer, depth, prev_nat, tm):
    bsz, s, d = x.shape
    nt = s // tm
    assert tm % CONV_HALO == 0
    tile = lambda width: pl.BlockSpec((1, tm, width), lambda b, i: (b, i, 0))
    halo_spec = pl.BlockSpec(
        (1, CONV_HALO, CONV_CH), lambda b, i: (b, jnp.maximum(i * (tm // CONV_HALO) - 1, 0), 0))
    nat_specs, nat_shapes = [], []
    for g in range(N_GROUPS):
        win = min(WINDOWS[g], s)
        assert win % tm == 0 or tm % win == 0
        nat_shapes.append(jax.ShapeDtypeStruct((depth * bsz, KV_WIDTH, win), F32))
        if win >= tm:
            first = (s - win) // tm
            nat_specs.append(pl.BlockSpec(
                (1, KV_WIDTH, tm),
                lambda b, i, first=first: (layer * bsz + b, 0, jnp.maximum(i - first, 0))))
        else:
            nat_specs.append(pl.BlockSpec((1, KV_WIDTH, win), lambda b, i: (layer * bsz + b, 0, 0)))
    args = [x, mod, w_in, b_in, glu, glu, *conv_w]
    in_specs = [tile(d), pl.BlockSpec((1, 1, MOD_WIDTH), lambda b, i: (b, 0, 0)),
                _layer_spec(w_in, layer), _layer_spec(b_in, layer),
                tile(CONV_CH), halo_spec] + [_layer_spec(w, layer) for w in conv_w]
    aliases = {}
    if prev_nat is not None:
        for g in range(N_GROUPS):
            in_specs.append(pl.BlockSpec(memory_space=pl.ANY))
            aliases[len(args)] = N_GROUPS + g
            args.append(prev_nat[g])
    outs = pl.pallas_call(
        functools.partial(_inproj_b_kernel, tm=tm, seq=s, aliased=prev_nat is not None),
        grid=(bsz, nt),
        in_specs=in_specs,
        out_specs=[_residue_spec(tm, dil, KV_WIDTH) for dil in DILATIONS] + nat_specs + [tile(CONV_CH)],
        out_shape=[jax.ShapeDtypeStruct((bsz, dil, s // dil, KV_WIDTH), BF16) for dil in DILATIONS]
                  + nat_shapes + [jax.ShapeDtypeStruct((bsz, s, CONV_CH), BF16)],
        scratch_shapes=[pltpu.VMEM((KV_WIDTH // LANES, tm, LANES), F32),
                        pltpu.VMEM((CONV_CH // LANES, CONV_HALO + tm, LANES), F32),
                        pltpu.VMEM((2 * N_GROUPS, tm, ATT_WIDTH), F32)],
        input_output_aliases=aliases,
        compiler_params=_params(2),
        name="inproj_b",
    )(*args)
    return outs[:N_GROUPS], outs[N_GROUPS:2 * N_GROUPS], outs[2 * N_GROUPS]


def _inproj_s_kernel(x_ref, mod_ref, xq_ref, modq_ref, w_ref, b_ref,
                     glu_ref, gc_ref, ga_ref, q_ref, kvt_ref):
    u = _modulated(x_ref, mod_ref, 0, 1)
    glu, gate_c, gate_a, qs = _glu_gates_q(u, w_ref, b_ref)
    glu_ref[0] = glu
    gc_ref[0] = gate_c.astype(gc_ref.dtype)
    ga_ref[0] = gate_a.astype(ga_ref.dtype)
    for g, q in enumerate(qs):
        q_ref[0, :, g * ATT_WIDTH:(g + 1) * ATT_WIDTH] = q
    uq = _modulated(xq_ref, modq_ref, 0, 1)
    for g in range(N_GROUPS):
        for half, c0 in enumerate((_k_col(g), _v_col(g))):
            r0 = g * KV_WIDTH + half * ATT_WIDTH
            kvt_ref[r0:r0 + ATT_WIDTH, :] = _proj(uq, w_ref, b_ref, c0, ATT_WIDTH).T


def _inproj_s(x, mod, xq, modq, w_in, b_in, layer):
    _, m, d = x.shape
    full = lambda *shape: _const_spec(shape)
    whole = lambda *shape: pl.BlockSpec(shape, lambda i: (0,) * len(shape))
    return pl.pallas_call(
        _inproj_s_kernel,
        in_specs=[full(1, m, d), full(1, m, MOD_WIDTH), full(1, m, d), full(1, m, MOD_WIDTH),
                  _layer_spec(w_in, layer), _layer_spec(b_in, layer)],
        out_specs=[whole(1, m, CONV_CH), whole(1, m, d), whole(1, m, d),
                   whole(1, m, N_GROUPS * ATT_WIDTH), whole(N_GROUPS * KV_WIDTH, m)],
        out_shape=[jax.ShapeDtypeStruct((1, m, CONV_CH), F32),
                   jax.ShapeDtypeStruct((1, m, d), GATE_DTYPE),
                   jax.ShapeDtypeStruct((1, m, d), GATE_DTYPE),
                   jax.ShapeDtypeStruct((1, m, N_GROUPS * ATT_WIDTH), F32),
                   jax.ShapeDtypeStruct((N_GROUPS * KV_WIDTH, m), F32)],
        grid=(1,),
        compiler_params=_params(1),
        name="inproj_s",
    )(x, mod, xq, modq, w_in, b_in)


def _alibi_slopes():
    return (2.0 ** (-8.0 * (np.arange(N_SLOTS) + 1) / N_SLOTS)).astype(np.float32)


def _prompt_bias(dil, key_blocks):
    qi = np.arange(BLOCK)[:, None]
    kj = np.arange(key_blocks * BLOCK)[None, :]
    tables = []
    for lead in (0, key_blocks - 1):
        rel = lead * BLOCK + qi - kj
        valid = (rel >= 0) & (rel <= SPAN)
        bias = -_alibi_slopes()[:, None, None] * (rel * dil).astype(np.float32)[None]
        tables.append(np.where(valid[None], bias, np.float32(NEG)))
    return np.stack(tables).astype(np.float32)


def _att_split(dil):
    f1 = min(dil, 4)
    assert dil % f1 == 0
    return f1, dil // f1


def _attn_kernel(q_ref, kv_ref, bias_ref, o_ref, lse_ref, s_scr, m_scr, *, nb, n_units, dil):
    kw = bias_ref.shape[-1]
    seq = o_ref.shape[2]
    lane = lax.broadcasted_iota(jnp.int32, (BLOCK, LANES), 1)
    pick = lambda a, b: jnp.where(lane < HEAD_DIM, a, b)

    def place(u):
        if isinstance(u, int):
            rr, c = divmod(u, nb)
            r0 = c * BLOCK
            k0 = max(r0 - (kw - BLOCK), 0)
            return rr, r0, k0, (0 if c == 0 else 1)
        rr = u // nb
        c = u % nb
        r0 = pl.multiple_of(c * BLOCK, BLOCK)
        k0 = pl.multiple_of(jnp.maximum(r0 - (kw - BLOCK), 0), BLOCK)
        return rr, r0, k0, jnp.minimum(c, 1)

    ones = jnp.ones((kw, LANES), BF16)

    def scores(u, slot):
        rr, r0, k0, tbl = place(u)
        for p in range(ATT_WIDTH // LANES):
            kp = kv_ref[0, rr, pl.ds(k0, kw), p * LANES:(p + 1) * LANES]
            q2 = q_ref[0, rr, pl.ds(r0, BLOCK), 2 * p * LANES:(2 * p + 2) * LANES]
            q2 = jnp.concatenate([q2[:, :LANES], q2[:, LANES:]], axis=0)
            bias2 = jnp.concatenate([bias_ref[tbl, 2 * p], bias_ref[tbl, 2 * p + 1]], axis=0)
            s = _dot_nt(q2, kp) + bias2
            m = jnp.max(s, axis=-1, keepdims=True)
            s_scr[slot, p] = jnp.exp(s - m).astype(BF16)
            m_scr[slot, p] = m

    def finish(u, slot):
        rr, r0, k0, _ = place(u)
        for p in range(ATT_WIDTH // LANES):
            vp = kv_ref[0, rr, pl.ds(k0, kw), ATT_WIDTH + p * LANES:ATT_WIDTH + (p + 1) * LANES]
            m = m_scr[slot, p]
            pv = _dot(s_scr[slot, p], jnp.concatenate([vp, ones], axis=1))
            l2 = pick(pv[:BLOCK, LANES:], pv[BLOCK:, LANES:])
            f1, f2 = _att_split(dil)
            if f1 == 1:
                rows = pl.ds(r0, BLOCK)
            else:
                rows = pl.ds((rr % f2) * (seq // f2) + r0 * f1 + rr // f2, BLOCK, stride=f1)
            o_ref[0, p, rows, :] = pick(pv[:BLOCK, :LANES], pv[BLOCK:, :LANES]) / l2
            lse_ref[0, p, rows, :] = pick(m[:BLOCK], m[BLOCK:]) + jnp.log(l2)

    scores(0, 0)

    def body(u, carry):
        slot = u % 2
        finish(u, slot)
        scores(u + 1, 1 - slot)
        return carry
    lax.fori_loop(0, n_units - 1, body, 0)
    finish(n_units - 1, (n_units - 1) % 2)


def _prompt_attention(qx, kvb, dil):
    bsz, _, n, _ = qx.shape
    assert n % BLOCK == 0
    nb = n // BLOCK
    bias = jnp.asarray(_prompt_bias(dil, min(nb, 2)))
    sub = lambda width: pl.BlockSpec((1, dil, n, width), lambda b: (b, 0, 0, 0))
    slabs = pl.BlockSpec((1, ATT_WIDTH // LANES, dil * n, LANES), lambda b: (b, 0, 0, 0))
    return pl.pallas_call(
        functools.partial(_attn_kernel, nb=nb, n_units=dil * nb, dil=dil),
        grid=(bsz,),
        in_specs=[sub(2 * ATT_WIDTH), sub(KV_WIDTH), _const_spec(bias.shape)],
        out_specs=[slabs, slabs],
        out_shape=[jax.ShapeDtypeStruct((bsz, ATT_WIDTH // LANES, dil * n, LANES), F32)] * 2,
        scratch_shapes=[pltpu.VMEM((2, N_SLOTS // 2, 2 * BLOCK, bias.shape[-1]), BF16),
                        pltpu.VMEM((2, N_SLOTS // 2, 2 * BLOCK, 1), F32)],
        compiler_params=_params(1),
        name=f"attn_d{dil}",
    )(qx, kvb, bias)


def _combine_groups(os_, lses):
    m = functools.reduce(jnp.maximum, lses)
    ws = [jnp.exp(l - m) for l in lses]
    den = functools.reduce(lambda a, b: a + b, ws)
    num = functools.reduce(lambda a, b: a + b, [w * o for w, o in zip(ws, os_)])
    return num / den


def _merge_to_ln1(x, m, ybuf, o_att, gate_c, gate_a,
                  wpc_ref, bpc_ref, wpa_ref, bpa_ref, wout_ref, bout_ref, g1_ref, b1_ref, alpha):
    y_conv = _dot(ybuf, wpc_ref[0]) + bpc_ref[0]
    y_att = _dot(o_att.astype(BF16), wpa_ref[0]) + bpa_ref[0]
    y = (gate_c * y_conv + gate_a * y_att).astype(BF16)
    y = _dot(y, wout_ref[0]) + bout_ref[0]
    return _layer_norm(alpha * x + (1.0 + _mod_slice(m, 2)) * y, g1_ref[0], b1_ref[0])


def _ffn_to_ln2(x1, m, wg_ref, wu_ref, wd_ref, g2_ref, b2_ref, alpha, chunk):
    u2 = (x1 * (1.0 + _mod_slice(m, 4)) + _mod_slice(m, 3)).astype(BF16)
    d_ff = wg_ref.shape[-1]
    h = None
    for c0 in range(0, d_ff, chunk):
        c1 = min(c0 + chunk, d_ff)
        a = (_silu(_dot(u2, wg_ref[0, :, c0:c1])) * _dot(u2, wu_ref[0, :, c0:c1])).astype(BF16)
        part = _dot(a, wd_ref[0, c0:c1, :])
        h = part if h is None else h + part
    return _layer_norm(alpha * x1 + (1.0 + _mod_slice(m, 5)) * h, g2_ref[0], b2_ref[0])


def _token_order(ref, scr_ref, slot, p):
    f2, n = ref.shape[2], ref.shape[3]
    if f2 == 1:
        return ref[0, p, 0]
    for b in range(f2):
        scr_ref[slot, pl.ds(b, n, stride=f2), :] = ref[0, p, b]
    return scr_ref[slot]


def _post_a_kernel(x_ref, mod_ref, ycv_ref, o0_ref, o1_ref, o2_ref, l0_ref, l1_ref, l2_ref,
                   gc_ref, ga_ref, wpc_ref, bpc_ref, wpa_ref, bpa_ref, wout_ref, bout_ref,
                   g1_ref, b1_ref, out_ref, scr_ref, *, alpha):
    o_refs = (o0_ref, o1_ref, o2_ref)
    l_refs = (l0_ref, l1_ref, l2_ref)
    chunks = []
    for p in range(ATT_WIDTH // LANES):
        os_ = [_token_order(o_refs[g], scr_ref, 2 * g, p) for g in range(N_GROUPS)]
        lses = [_token_order(l_refs[g], scr_ref, 2 * g + 1, p) for g in range(N_GROUPS)]
        chunks.append(_combine_groups(os_, lses).astype(BF16))
    o_att = jnp.concatenate(chunks, axis=1)
    out_ref[0] = _merge_to_ln1(x_ref[0], mod_ref[0], ycv_ref[0], o_att,
                               gc_ref[0].astype(F32), ga_ref[0].astype(F32),
                               wpc_ref, bpc_ref, wpa_ref, bpa_ref, wout_ref, bout_ref,
                               g1_ref, b1_ref, alpha)


def _post_a(x, mod, ycv, os_, lses, gate_c, gate_a, wts, layer, tm, alpha):
    bsz, s, d = x.shape
    nt = s // tm
    tile = lambda width: pl.BlockSpec((1, tm, width), lambda b, i: (b, i, 0))
    nslab = ATT_WIDTH // LANES
    splits = [_att_split(dil)[1] for dil in DILATIONS]
    slab_specs = [pl.BlockSpec((1, nslab, f2, tm // f2, LANES), lambda b, i: (b, 0, 0, i, 0)) for f2 in splits]
    view = lambda arrs: [a.reshape(bsz, nslab, f2, s // f2, LANES) for a, f2 in zip(arrs, splits)]
    return pl.pallas_call(
        functools.partial(_post_a_kernel, alpha=alpha),
        grid=(bsz, nt),
        in_specs=([tile(d), pl.BlockSpec((1, 1, MOD_WIDTH), lambda b, i: (b, 0, 0)), tile(CONV_CH)]
                  + slab_specs + slab_specs + [tile(d), tile(d)]
                  + [_layer_spec(w, layer) for w in wts]),
        out_specs=tile(d),
        out_shape=jax.ShapeDtypeStruct((bsz, s, d), F32),
        scratch_shapes=[pltpu.VMEM((2 * N_GROUPS, tm, LANES), F32)],
        compiler_params=_params(2),
        name="post_a",
    )(x, mod, ycv, *view(os_), *view(lses), gate_c, gate_a, *wts)


def _post_b_kernel(*refs, alpha, chunk, t_new, nsq, nblk):
    x_ref, mod_ref, wg_ref, wu_ref, wd_ref, g2_ref, b2_ref = refs[:7]
    q_refs, c_refs, n_refs = refs[7:10], refs[10:13], refs[13:16]
    bc_refs, bn_refs = refs[16:19], refs[19:22]
    out_ref, o_ref = refs[-5:-3]
    oc_refs = refs[-3:]
    out_ref[0] = _ffn_to_ln2(x_ref[0], mod_ref[0], wg_ref, wu_ref, wd_ref, g2_ref, b2_ref, alpha, chunk)
    step = pl.program_id(0) * pl.num_programs(1) + pl.program_id(1)
    first_seq = (step % nblk) * nsq
    for i in range(nsq):
        _sample_unit(q_refs, c_refs, n_refs, bc_refs, bn_refs, o_ref, oc_refs, i, first_seq + i, t_new)


def _post_b(x1, mod, wts, layer, tm, alpha, q8, caches_n, kvt_new, prev_outs, nseq, t_new):
    bsz, s, d = x1.shape
    nt = s // tm
    hsteps = ATT_WIDTH // LANES
    steps = bsz * nt
    assert (hsteps * nseq) % steps == 0 and nseq * t_new == LANES
    nsq = hsteps * nseq // steps
    nblk = nseq // nsq
    side = lambda f: (lambda b, i: f(*divmod(b * nt + i, nblk)))
    tables = [_sample_bias(g, caches_n[g].shape[-1], t_new) for g in range(N_GROUPS)]
    bc = [jnp.asarray(tc.reshape(hsteps, 2, Q_ROWS, -1)) for tc, _ in tables]
    bn = [jnp.asarray(tn.reshape(hsteps, 2, Q_ROWS, LANES)) for _, tn in tables]
    kvt4 = kvt_new.reshape(N_GROUPS, 2, ATT_WIDTH, nseq * t_new)
    tile = pl.BlockSpec((1, tm, d), lambda b, i: (b, i, 0))
    args = [x1, mod, *wts]
    in_specs = ([tile, pl.BlockSpec((1, 1, MOD_WIDTH), lambda b, i: (b, 0, 0))]
                + [_layer_spec(w, layer) for w in wts])
    for g in range(N_GROUPS):
        in_specs.append(pl.BlockSpec((nsq, Q_ROWS, LANES), side(lambda hc, blk, g=g: (blk, 0, g * hsteps + hc))))
        args.append(q8)
    cache_specs = [pl.BlockSpec((nsq, 2, LANES, c.shape[-1]),
                                side(lambda hc, blk: (layer * nblk + blk, 0, hc, 0))) for c in caches_n]
    in_specs += cache_specs
    args += list(caches_n)
    for g in range(N_GROUPS):
        in_specs.append(pl.BlockSpec((1, 2, LANES, LANES), side(lambda hc, blk, g=g: (g, 0, hc, 0))))
        args.append(kvt4)
    for tbl in bc + bn:
        in_specs.append(pl.BlockSpec((1,) + tbl.shape[1:], side(lambda hc, blk: (hc, 0, 0, 0))))
        args.append(tbl)
    aliases = {}
    if prev_outs is not None:
        for g in range(N_GROUPS):
            in_specs.append(pl.BlockSpec(memory_space=pl.ANY))
            aliases[len(args)] = 2 + g
            args.append(prev_outs[g])
    outs = pl.pallas_call(
        functools.partial(_post_b_kernel, alpha=alpha, chunk=1024, t_new=t_new, nsq=nsq, nblk=nblk),
        grid=(bsz, nt),
        in_specs=in_specs,
        out_specs=[tile, pl.BlockSpec((nsq, Q_ROWS, LANES), side(lambda hc, blk: (blk, 0, hc)))] + cache_specs,
        out_shape=[jax.ShapeDtypeStruct((bsz, s, d), F32),
                   jax.ShapeDtypeStruct((nseq, Q_ROWS, ATT_WIDTH), F32)]
                  + [jax.ShapeDtypeStruct(c.shape, F32) for c in caches_n],
        input_output_aliases=aliases,
        compiler_params=_params(2),
        name="post_b",
    )(*args)
    return outs[0], outs[1], outs[2:]


def _post_s_kernel(x_ref, mod_ref, state_ref, glu_ref, o_ref, gc_ref, ga_ref,
                   wdw_ref, bdw_ref, cg_ref, cb_ref,
                   wpc_ref, bpc_ref, wpa_ref, bpa_ref, wout_ref, bout_ref, g1_ref, b1_ref,
                   wg_ref, wu_ref, wd_ref, g2_ref, b2_ref,
                   out_ref, nconv_ref, ypre_ref, *, nseq, t_new, alpha, chunk):
    ctx = CONV_WIDTH - 1

    def slab(i):
        if i < ctx:
            return state_ref[0, i]
        return glu_ref[0, (i - ctx) * nseq:(i - ctx + 1) * nseq, :]

    for t in range(t_new):
        acc = jnp.zeros((nseq, CONV_CH), F32) + bdw_ref[0]
        for j in range(CONV_WIDTH):
            acc = acc + slab(t + j) * wdw_ref[0, j:j + 1, :]
        ypre_ref[t * nseq:(t + 1) * nseq, :] = acc
    for i in range(ctx):
        nconv_ref[0, i] = slab(i + t_new)
    ybuf = _silu(_layer_norm(ypre_ref[...], cg_ref[0], cb_ref[0])).astype(BF16)
    m = mod_ref[0]
    x1 = _merge_to_ln1(x_ref[0], m, ybuf, o_ref[0], gc_ref[0].astype(F32), ga_ref[0].astype(F32),
                       wpc_ref, bpc_ref, wpa_ref, bpa_ref, wout_ref, bout_ref, g1_ref, b1_ref, alpha)
    out_ref[0] = _ffn_to_ln2(x1, m, wg_ref, wu_ref, wd_ref, g2_ref, b2_ref, alpha, chunk)


def _post_s(x, mod, state_n, layer, glu, o_att, gate_c, gate_a, wts, nseq, t_new, alpha):
    _, m, d = x.shape
    ctx = CONV_WIDTH - 1
    full = lambda *shape: _const_spec(shape)
    whole = lambda *shape: pl.BlockSpec(shape, lambda i: (0,) * len(shape))
    return pl.pallas_call(
        functools.partial(_post_s_kernel, nseq=nseq, t_new=t_new, alpha=alpha, chunk=1024),
        in_specs=[full(1, m, d), full(1, m, MOD_WIDTH), _layer_spec(state_n, layer), full(1, m, CONV_CH),
                  full(1, m, ATT_WIDTH), full(1, m, d), full(1, m, d)]
                 + [_layer_spec(w, layer) for w in wts],
        out_specs=[whole(1, m, d), whole(1, ctx, nseq, CONV_CH)],
        out_shape=[jax.ShapeDtypeStruct((1, m, d), F32),
                   jax.ShapeDtypeStruct((1, ctx, nseq, CONV_CH), F32)],
        scratch_shapes=[pltpu.VMEM((m, CONV_CH), F32)],
        grid=(1,),
        compiler_params=_params(1),
        name="post_s",
    )(x, mod, state_n, glu, o_att, gate_c, gate_a, *wts)


def _sample_bias(g, length, t_new):
    dil = DILATIONS[g]
    slopes = _alibi_slopes()[:, None, None]
    t = np.arange(Q_ROWS)[:, None]
    p = np.arange(length)[None, :]
    dist = length + t - p
    valid = (dist % dil == 0) & (dist // dil >= 1) & (dist // dil <= SPAN) & (t < t_new)
    cache = np.where(valid[None], -slopes * dist.astype(np.float32)[None], np.float32(NEG))
    tp = np.arange(LANES)[None, :] - (LANES - t_new)
    dist_n = t - tp
    valid_n = (tp >= 0) & (dist_n >= 0) & (dist_n % dil == 0) & (dist_n // dil <= SPAN) & (t < t_new)
    new = np.where(valid_n[None], -slopes * dist_n.astype(np.float32)[None], np.float32(NEG))
    return cache.astype(np.float32), new.astype(np.float32)


def _sample_unit(q_refs, c_refs, n_refs, bc_refs, bn_refs, o_ref, oc_refs, i, seq, t_new):
    shift = (LANES - t_new) - t_new * seq
    lane = lax.broadcasted_iota(jnp.int32, (LANES, LANES), 1)
    heads_per_step = LANES // HEAD_DIM
    per_head = [[] for _ in range(heads_per_step)]
    for g in range(N_GROUPS):
        c_ref = c_refs[g]
        length = c_ref.shape[-1]
        new = [pltpu.roll(n_refs[g][0, kv], shift, axis=1) for kv in range(2)]
        for e in range(heads_per_step):
            rows = slice(e * HEAD_DIM, (e + 1) * HEAD_DIM)
            qh = q_refs[g][i, :, rows].astype(BF16)
            s_c = _dot(qh, c_ref[i, 0, rows, :].astype(BF16)) + bc_refs[g][0, e]
            s_n = _dot(qh, new[0][rows, :].astype(BF16)) + bn_refs[g][0, e]
            m = jnp.maximum(jnp.max(s_c, axis=-1, keepdims=True), jnp.max(s_n, axis=-1, keepdims=True))
            p_c = jnp.exp(s_c - m)
            p_n = jnp.exp(s_n - m)
            l = jnp.sum(p_c, axis=-1, keepdims=True) + jnp.sum(p_n, axis=-1, keepdims=True)
            o = (_dot_nt(p_c.astype(BF16), c_ref[i, 1, rows, :].astype(BF16))
                 + _dot_nt(p_n.astype(BF16), new[1][rows, :].astype(BF16)))
            per_head[e].append((o / l, m + jnp.log(l)))
        for kv in range(2):
            rolled = pltpu.roll(c_ref[i, kv], length - t_new, axis=1)
            if length > LANES:
                oc_refs[g][i, kv, :, 0:length - LANES] = rolled[:, 0:length - LANES]
            oc_refs[g][i, kv, :, length - LANES:length] = jnp.where(
                lane >= LANES - t_new, new[kv], rolled[:, length - LANES:length])
    for e in range(heads_per_step):
        os_, lses = zip(*per_head[e])
        o_ref[i, :, e * HEAD_DIM:(e + 1) * HEAD_DIM] = _combine_groups(list(os_), list(lses))


def _native_view(cache):
    depth, nseq, length = cache.shape[:3]
    return cache.transpose(0, 1, 3, 4, 5, 2).reshape(depth * nseq, 2, ATT_WIDTH, length)


def _from_native(x, depth, nseq):
    length = x.shape[-1]
    return x.reshape(depth, nseq, 2, N_SLOTS, HEAD_DIM, length).transpose(0, 1, 5, 2, 3, 4)


def kernel(x_prompt, x_sample, c_prompt, c_sample, cache_kv_g0, cache_kv_g1, cache_kv_g2, state_conv, w_ada, b_ada, w_in, b_in, w_dw, b_dw, conv_ln_g, conv_ln_b, w_pc, b_pc, w_pa, b_pa, w_out, b_out, ln1_g, ln1_b, w_gate, w_up, w_down, ln2_g, ln2_b):
    depth = w_in.shape[0]
    bsz, seq, d = x_prompt.shape
    nseq, t_new, _ = x_sample.shape
    alpha = (2 * depth) ** 0.25
    tm = 512
    tm_a = 1024
    tm_ffn = 256
    m_s = nseq * t_new

    mod_all = _adaln(jnp.concatenate([c_prompt, c_sample], axis=0), w_ada, b_ada)
    caches_n = [_native_view(c) for c in (cache_kv_g0, cache_kv_g1, cache_kv_g2)]
    state_n = state_conv.transpose(0, 2, 1, 3)

    vec = lambda v: v.reshape(depth, 1, -1)
    w_in_bf = w_in.astype(BF16)
    b_in3 = vec(b_in)
    conv_w = [w_dw, vec(b_dw), vec(conv_ln_g), vec(conv_ln_b)]
    merge_w = [w_pc.astype(BF16), vec(b_pc), w_pa.astype(BF16), vec(b_pa),
               w_out.astype(BF16), vec(b_out), vec(ln1_g), vec(ln1_b)]
    ffn_w = [w_gate.astype(BF16), w_up.astype(BF16), w_down.astype(BF16), vec(ln2_g), vec(ln2_b)]

    xp = x_prompt
    xs = x_sample.transpose(1, 0, 2).reshape(1, m_s, d)
    conv_prompt, conv_sample = [], []
    cache_outs = None
    kv_nat = None
    for l in range(depth):
        mod_p = mod_all[l, :bsz].reshape(bsz, 1, MOD_WIDTH)
        mod_s = jnp.tile(mod_all[l, bsz:], (t_new, 1)).reshape(1, m_s, MOD_WIDTH)
        mod_sq = jnp.repeat(mod_all[l, bsz:], t_new, axis=0).reshape(1, m_s, MOD_WIDTH)

        xs_q = xs.reshape(t_new, nseq, d).transpose(1, 0, 2).reshape(1, m_s, d)
        glu_s, gc_s, ga_s, q_s, kvt_new = _inproj_s(xs, mod_s, xs_q, mod_sq, w_in_bf, b_in3, l)
        q8 = jnp.pad(q_s.reshape(t_new, nseq, N_GROUPS * ATT_WIDTH).transpose(1, 0, 2),
                     ((0, 0), (0, Q_ROWS - t_new), (0, 0)))

        glu, gate_c, gate_a, *qx = _inproj_a(xp, mod_p, w_in_bf, b_in3, l, tm_a)
        kvb, kv_nat, ycv = _inproj_b(xp, mod_p, w_in_bf, b_in3, glu, conv_w, l, depth, kv_nat, tm)
        conv_prompt.append(glu[:, seq - (CONV_WIDTH - 1):, :])
        os_, lses = [], []
        for g in range(N_GROUPS):
            o, lse = _prompt_attention(qx[g], kvb[g], DILATIONS[g])
            os_.append(o)
            lses.append(lse)
        x1 = _post_a(xp, mod_p, ycv, os_, lses, gate_c, gate_a, merge_w, l, tm, alpha)
        xp, o8, cache_outs = _post_b(x1, mod_p, ffn_w, l, tm_ffn, alpha,
                                     q8, caches_n, kvt_new, cache_outs, nseq, t_new)

        o_s = o8[:, :t_new].transpose(1, 0, 2).reshape(1, m_s, ATT_WIDTH)
        xs, nconv = _post_s(xs, mod_s, state_n, l, glu_s, o_s, gc_s, ga_s, conv_w + merge_w + ffn_w,
                            nseq, t_new, alpha)
        conv_sample.append(nconv[0])

    kv_p = [_from_native(kv_nat[g].reshape(depth * bsz, 2, ATT_WIDTH, -1), depth, bsz)
            for g in range(N_GROUPS)]
    kv_s = [_from_native(cache_outs[g], depth, nseq) for g in range(N_GROUPS)]
    return (xp, xs.reshape(t_new, nseq, d).transpose(1, 0, 2),
            kv_p[0], kv_p[1], kv_p[2], jnp.stack(conv_prompt),
            kv_s[0], kv_s[1], kv_s[2], jnp.stack(conv_sample).transpose(0, 2, 1, 3))
```

```python
import functools
from typing import NamedTuple

import numpy as np
import jax
import jax.numpy as jnp
from jax import lax
from jax.experimental import pallas as pl
from jax.experimental.pallas import tpu as pltpu

D_MODEL = 1024
CONV_CH = 512
CONV_WIDTH = 31
N_SLOTS = 8
HEAD_DIM = 64
ATT_WIDTH = N_SLOTS * HEAD_DIM
WINDOWS = (128, 512, 2048)
DILATIONS = (1, 4, 16)
N_GROUPS = 3
SPAN = 128
BLOCK = 128
LN_EPS = 1e-5
QKV_OFF = 2 * CONV_CH
GATE_OFF = QKV_OFF + 3 * N_GROUPS * ATT_WIDTH
IN_WIDTH = GATE_OFF + 2 * D_MODEL
KV_WIDTH = 2 * ATT_WIDTH
MOD_WIDTH = 6 * D_MODEL
LANES = 128
NEG = -1e30
Q_ROWS = 8
CONV_HALO = 32
VMEM_LIMIT = 56 * 1024 * 1024

F32 = jnp.float32
BF16 = jnp.bfloat16
GATE_DTYPE = jnp.bfloat16

_NT = (((1,), (1,)), ((), ()))


def _q_col(g):
    return QKV_OFF + g * ATT_WIDTH


def _k_col(g):
    return QKV_OFF + (N_GROUPS + g) * ATT_WIDTH


def _v_col(g):
    return QKV_OFF + (2 * N_GROUPS + g) * ATT_WIDTH


def _dot(a, b):
    return jnp.dot(a, b, preferred_element_type=F32)


def _dot_nt(a, b):
    return lax.dot_general(a, b, _NT, preferred_element_type=F32)


def _sigmoid(x):
    return 1.0 / (1.0 + jnp.exp(-x))


def _silu(x):
    return x * _sigmoid(x)


def _layer_norm(x, g, b):
    mu = jnp.mean(x, axis=-1, keepdims=True)
    xc = x - mu
    var = jnp.mean(xc * xc, axis=-1, keepdims=True)
    return xc * lax.rsqrt(var + LN_EPS) * g + b


def _mod_slice(m, i):
    return m[:, i * D_MODEL:(i + 1) * D_MODEL]


def _const_spec(shape):
    nd = len(shape)
    return pl.BlockSpec(shape, lambda *_: (0,) * nd, pipeline_mode=pl.Buffered(1))


class _Vec(NamedTuple):
    packed: jax.Array
    offset: int
    width: int


def _pack_vectors(vectors):
    depth = vectors[0].shape[0]
    packed = jnp.concatenate(vectors, axis=1).reshape(depth, 1, -1)
    out, offset = [], 0
    for v in vectors:
        width = v.shape[1]
        assert offset % width == 0
        out.append(_Vec(packed, offset, width))
        offset += width
    return out


def _arg(p):
    return p.packed if isinstance(p, _Vec) else p


def _layer_spec(p, layer):
    if isinstance(p, _Vec):
        return pl.BlockSpec((1, 1, p.width), lambda *_: (layer, 0, p.offset // p.width),
                            pipeline_mode=pl.Buffered(1))
    nd = p.ndim
    return pl.BlockSpec((1,) + p.shape[1:], lambda *_: (layer,) + (0,) * (nd - 1),
                        pipeline_mode=pl.Buffered(1))


def _params(n_axes):
    return pltpu.CompilerParams(dimension_semantics=("arbitrary",) * n_axes,
                                vmem_limit_bytes=VMEM_LIMIT)


def _adaln_kernel(c_ref, w_ref, b_ref, o_ref):
    c = c_ref[...]
    s = _silu(c).astype(BF16)
    o_ref[0] = _dot(s, w_ref[0].astype(BF16)) + b_ref[0]


def _adaln(c_all, w_ada, b_ada):
    depth, d, width = w_ada.shape
    rows = c_all.shape[0]
    tn = 1024
    return pl.pallas_call(
        _adaln_kernel,
        grid=(depth, width // tn),
        in_specs=[pl.BlockSpec((rows, d), lambda l, j: (0, 0)),
                  pl.BlockSpec((1, d, tn), lambda l, j: (l, 0, j)),
                  pl.BlockSpec((1, 1, tn), lambda l, j: (l, 0, j))],
        out_specs=pl.BlockSpec((1, rows, tn), lambda l, j: (l, 0, j)),
        out_shape=jax.ShapeDtypeStruct((depth, rows, width), F32),
        compiler_params=_params(2),
        name="adaln",
    )(c_all, w_ada, b_ada.reshape(depth, 1, width))


def _modulated(x_ref, mod_ref, shift_i, scale_i):
    m = mod_ref[0]
    x = x_ref[0]
    return (x * (1.0 + _mod_slice(m, scale_i)) + _mod_slice(m, shift_i)).astype(BF16)


def _proj(u, w_ref, b_ref, c0, width):
    return _dot(u, w_ref[0, :, c0:c0 + width]) + b_ref[0, :, c0:c0 + width]


def _glu_gates_q(u, w_ref, b_ref):
    glu = _proj(u, w_ref, b_ref, 0, CONV_CH) * _sigmoid(_proj(u, w_ref, b_ref, CONV_CH, CONV_CH))
    gate_c = _sigmoid(_proj(u, w_ref, b_ref, GATE_OFF, D_MODEL))
    gate_a = _sigmoid(_proj(u, w_ref, b_ref, GATE_OFF + D_MODEL, D_MODEL))
    qs = [_proj(u, w_ref, b_ref, _q_col(g), ATT_WIDTH) * (HEAD_DIM ** -0.5) for g in range(N_GROUPS)]
    return glu, gate_c, gate_a, qs


def _residue_rows(t, scr_ref, slot, dil):
    if dil == 1:
        return [t]
    rows = t.shape[0]
    scr_ref[slot] = t
    return [scr_ref[slot, pl.ds(r, rows // dil, stride=dil), :] for r in range(dil)]


def _inproj_a_kernel(x_ref, mod_ref, w_ref, b_ref, glu_ref, gc_ref, ga_ref, q0_ref, q1_ref, q2_ref,
                     scr_ref):
    u = _modulated(x_ref, mod_ref, 0, 1)
    glu, gate_c, gate_a, qs = _glu_gates_q(u, w_ref, b_ref)
    glu_ref[0] = glu
    gc_ref[0] = gate_c.astype(gc_ref.dtype)
    ga_ref[0] = gate_a.astype(ga_ref.dtype)
    for g, (q, q_ref) in enumerate(zip(qs, (q0_ref, q1_ref, q2_ref))):
        dil = DILATIONS[g]
        for p in range(ATT_WIDTH // LANES):
            parts = _residue_rows(q[:, p * LANES:(p + 1) * LANES], scr_ref, p, dil)
            for r, t in enumerate(parts):
                lane = lax.broadcasted_iota(jnp.int32, t.shape, 1)
                q_ref[0, r, :, (2 * p) * LANES:(2 * p + 1) * LANES] = jnp.where(lane < HEAD_DIM, t, 0.0).astype(BF16)
                q_ref[0, r, :, (2 * p + 1) * LANES:(2 * p + 2) * LANES] = jnp.where(lane >= HEAD_DIM, t, 0.0).astype(BF16)


def _residue_spec(tm, dil, width):
    return pl.BlockSpec((1, dil, tm // dil, width), lambda b, i: (b, 0, i, 0))


def _inproj_a(x, mod, w_in, b_in, layer, tm):
    bsz, s, d = x.shape
    nt = s // tm
    tile = lambda width: pl.BlockSpec((1, tm, width), lambda b, i: (b, i, 0))
    out_shape = ([jax.ShapeDtypeStruct((bsz, s, CONV_CH), F32),
                  jax.ShapeDtypeStruct((bsz, s, d), GATE_DTYPE),
                  jax.ShapeDtypeStruct((bsz, s, d), GATE_DTYPE)]
                 + [jax.ShapeDtypeStruct((bsz, dil, s // dil, 2 * ATT_WIDTH), BF16) for dil in DILATIONS])
    return pl.pallas_call(
        _inproj_a_kernel,
        grid=(bsz, nt),
        in_specs=[tile(d),
                  pl.BlockSpec((1, 1, MOD_WIDTH), lambda b, i: (b, 0, 0)),
                  _layer_spec(w_in, layer), _layer_spec(b_in, layer)],
        out_specs=[tile(CONV_CH), tile(d), tile(d)]
                  + [_residue_spec(tm, dil, 2 * ATT_WIDTH) for dil in DILATIONS],
        out_shape=out_shape,
        scratch_shapes=[pltpu.VMEM((ATT_WIDTH // LANES, tm, LANES), F32)],
        compiler_params=_params(2),
        name="inproj_a",
    )(x, mod, w_in, _arg(b_in))


def _conv_ln_silu(win_ref, r0, rb, wdw_ref, bdw_ref, cg_ref, cb_ref, out_ref):
    off = CONV_HALO - (CONV_WIDTH - 1)
    accs = []
    for c in range(CONV_CH // LANES):
        cols = slice(c * LANES, (c + 1) * LANES)
        acc = jnp.zeros((rb, LANES), F32) + bdw_ref[0, :, cols]
        for j in range(CONV_WIDTH):
            acc = acc + win_ref[c, pl.ds(r0 + off + j, rb, stride=1), :] * wdw_ref[0, j:j + 1, cols]
        accs.append(acc)
    y = _silu(_layer_norm(jnp.concatenate(accs, axis=1), cg_ref[0], cb_ref[0]))
    out_ref[0, r0:r0 + rb, :] = y.astype(out_ref.dtype)


def _inproj_b_kernel(*refs, tm, seq, aliased):
    (x_ref, mod_ref, w_ref, b_ref, glu_ref, halo_ref,
     wdw_ref, bdw_ref, cg_ref, cb_ref) = refs[:10]
    n_in = 10 + (N_GROUPS if aliased else 0)
    kv_refs = refs[n_in:n_in + N_GROUPS]
    nat_refs = refs[n_in + N_GROUPS:n_in + 2 * N_GROUPS]
    ycv_ref, scr_ref, win_ref, keep_ref = refs[n_in + 2 * N_GROUPS:]
    ti = pl.program_id(1)
    halo = halo_ref[0]
    halo = jnp.where(ti > 0, halo, jnp.zeros_like(halo))
    glu = glu_ref[0]
    for c in range(CONV_CH // LANES):
        win_ref[c, 0:CONV_HALO, :] = halo[:, c * LANES:(c + 1) * LANES]
        win_ref[c, CONV_HALO:CONV_HALO + tm, :] = glu[:, c * LANES:(c + 1) * LANES]
    u = _modulated(x_ref, mod_ref, 0, 1)
    wins = [min(w, seq) for w in WINDOWS]
    every_tile = [w >= seq for w in wins]

    rb = 64
    for r0 in range(0, tm, rb):
        _conv_ln_silu(win_ref, r0, rb, wdw_ref, bdw_ref, cg_ref, cb_ref, ycv_ref)
    for g, kv_ref in enumerate(kv_refs):
        for half, c0 in enumerate((_k_col(g), _v_col(g))):
            t = _proj(u, w_ref, b_ref, c0, ATT_WIDTH)
            for p in range(ATT_WIDTH // LANES):
                parts = _residue_rows(t[:, p * LANES:(p + 1) * LANES], scr_ref, half * 4 + p, DILATIONS[g])
                for r, part in enumerate(parts):
                    cols = slice(half * ATT_WIDTH + p * LANES, half * ATT_WIDTH + (p + 1) * LANES)
                    kv_ref[0, r, :, cols] = part.astype(BF16)
            if every_tile[g]:
                nat_refs[g][0, half * ATT_WIDTH:(half + 1) * ATT_WIDTH, :] = t.T
            else:
                keep_ref[2 * g + half] = t
    for g, n_ref in enumerate(nat_refs):
        if every_tile[g]:
            continue
        win = wins[g]
        rows = min(win, tm)
        cond = (ti >= (seq - win) // tm) if win >= tm else (ti == seq // tm - 1)

        @pl.when(cond)
        def _(g=g, n_ref=n_ref, rows=rows):
            for half in range(2):
                n_ref[0, half * ATT_WIDTH:(half + 1) * ATT_WIDTH, :] = keep_ref[2 * g + half, tm - rows:tm, :].T


def _inproj_b(x, mod, w_in, b_in, glu, conv_w, layer, depth, prev_nat, tm):
    bsz, s, d = x.shape
    nt = s // tm
    assert tm % CONV_HALO == 0
    tile = lambda width: pl.BlockSpec((1, tm, width), lambda b, i: (b, i, 0))
    halo_spec = pl.BlockSpec(
        (1, CONV_HALO, CONV_CH), lambda b, i: (b, jnp.maximum(i * (tm // CONV_HALO) - 1, 0), 0))
    nat_specs, nat_shapes = [], []
    for g in range(N_GROUPS):
        win = min(WINDOWS[g], s)
        assert win % tm == 0 or tm % win == 0
        nat_shapes.append(jax.ShapeDtypeStruct((depth * bsz, KV_WIDTH, win), F32))
        if win >= tm:
            first = (s - win) // tm
            nat_specs.append(pl.BlockSpec(
                (1, KV_WIDTH, tm),
                lambda b, i, first=first: (layer * bsz + b, 0, jnp.maximum(i - first, 0))))
        else:
            nat_specs.append(pl.BlockSpec((1, KV_WIDTH, win), lambda b, i: (layer * bsz + b, 0, 0)))
    args = [x, mod, w_in, _arg(b_in), glu, glu] + [_arg(w) for w in conv_w]
    in_specs = [tile(d), pl.BlockSpec((1, 1, MOD_WIDTH), lambda b, i: (b, 0, 0)),
                _layer_spec(w_in, layer), _layer_spec(b_in, layer),
                tile(CONV_CH), halo_spec] + [_layer_spec(w, layer) for w in conv_w]
    aliases = {}
    if prev_nat is not None:
        for g in range(N_GROUPS):
            in_specs.append(pl.BlockSpec(memory_space=pl.ANY))
            aliases[len(args)] = N_GROUPS + g
            args.append(prev_nat[g])
    outs = pl.pallas_call(
        functools.partial(_inproj_b_kernel, tm=tm, seq=s, aliased=prev_nat is not None),
        grid=(bsz, nt),
        in_specs=in_specs,
        out_specs=[_residue_spec(tm, dil, KV_WIDTH) for dil in DILATIONS] + nat_specs + [tile(CONV_CH)],
        out_shape=[jax.ShapeDtypeStruct((bsz, dil, s // dil, KV_WIDTH), BF16) for dil in DILATIONS]
                  + nat_shapes + [jax.ShapeDtypeStruct((bsz, s, CONV_CH), BF16)],
        scratch_shapes=[pltpu.VMEM((KV_WIDTH // LANES, tm, LANES), F32),
                        pltpu.VMEM((CONV_CH // LANES, CONV_HALO + tm, LANES), F32),
                        pltpu.VMEM((2 * N_GROUPS, tm, ATT_WIDTH), F32)],
        input_output_aliases=aliases,
        compiler_params=_params(2),
        name="inproj_b",
    )(*args)
    return outs[:N_GROUPS], outs[N_GROUPS:2 * N_GROUPS], outs[2 * N_GROUPS]


def _inproj_s_kernel(x_ref, mod_ref, xq_ref, modq_ref, w_ref, b_ref,
                     glu_ref, gc_ref, ga_ref, q_ref, kvt_ref):
    u = _modulated(x_ref, mod_ref, 0, 1)
    glu, gate_c, gate_a, qs = _glu_gates_q(u, w_ref, b_ref)
    glu_ref[0] = glu
    gc_ref[0] = gate_c.astype(gc_ref.dtype)
    ga_ref[0] = gate_a.astype(ga_ref.dtype)
    for g, q in enumerate(qs):
        q_ref[0, :, g * ATT_WIDTH:(g + 1) * ATT_WIDTH] = q
    uq = _modulated(xq_ref, modq_ref, 0, 1)
    for g in range(N_GROUPS):
        for half, c0 in enumerate((_k_col(g), _v_col(g))):
            r0 = g * KV_WIDTH + half * ATT_WIDTH
            kvt_ref[r0:r0 + ATT_WIDTH, :] = _proj(uq, w_ref, b_ref, c0, ATT_WIDTH).T


def _inproj_s(x, mod, xq, modq, w_in, b_in, layer):
    _, m, d = x.shape
    full = lambda *shape: _const_spec(shape)
    whole = lambda *shape: pl.BlockSpec(shape, lambda i: (0,) * len(shape))
    return pl.pallas_call(
        _inproj_s_kernel,
        in_specs=[full(1, m, d), full(1, m, MOD_WIDTH), full(1, m, d), full(1, m, MOD_WIDTH),
                  _layer_spec(w_in, layer), _layer_spec(b_in, layer)],
        out_specs=[whole(1, m, CONV_CH), whole(1, m, d), whole(1, m, d),
                   whole(1, m, N_GROUPS * ATT_WIDTH), whole(N_GROUPS * KV_WIDTH, m)],
        out_shape=[jax.ShapeDtypeStruct((1, m, CONV_CH), F32),
                   jax.ShapeDtypeStruct((1, m, d), GATE_DTYPE),
                   jax.ShapeDtypeStruct((1, m, d), GATE_DTYPE),
                   jax.ShapeDtypeStruct((1, m, N_GROUPS * ATT_WIDTH), F32),
                   jax.ShapeDtypeStruct((N_GROUPS * KV_WIDTH, m), F32)],
        grid=(1,),
        compiler_params=_params(1),
        name="inproj_s",
    )(x, mod, xq, modq, w_in, _arg(b_in))


def _alibi_slopes():
    return (2.0 ** (-8.0 * (np.arange(N_SLOTS) + 1) / N_SLOTS)).astype(np.float32)


def _prompt_bias(dil, key_blocks):
    qi = np.arange(BLOCK)[:, None]
    kj = np.arange(key_blocks * BLOCK)[None, :]
    tables = []
    for lead in (0, key_blocks - 1):
        rel = lead * BLOCK + qi - kj
        valid = (rel >= 0) & (rel <= SPAN)
        bias = -_alibi_slopes()[:, None, None] * (rel * dil).astype(np.float32)[None]
        tables.append(np.where(valid[None], bias, np.float32(NEG)))
    return np.stack(tables).astype(np.float32)


def _att_split(dil):
    f1 = min(dil, 4)
    assert dil % f1 == 0
    return f1, dil // f1


def _attn_kernel(q_ref, kv_ref, bias_ref, o_ref, lse_ref, s_scr, m_scr, *, nb, n_units, dil):
    kw = bias_ref.shape[-1]
    seq = o_ref.shape[2]
    lane = lax.broadcasted_iota(jnp.int32, (BLOCK, LANES), 1)
    pick = lambda a, b: jnp.where(lane < HEAD_DIM, a, b)

    def place(u):
        if isinstance(u, int):
            rr, c = divmod(u, nb)
            r0 = c * BLOCK
            k0 = max(r0 - (kw - BLOCK), 0)
            return rr, r0, k0, (0 if c == 0 else 1)
        rr = u // nb
        c = u % nb
        r0 = pl.multiple_of(c * BLOCK, BLOCK)
        k0 = pl.multiple_of(jnp.maximum(r0 - (kw - BLOCK), 0), BLOCK)
        return rr, r0, k0, jnp.minimum(c, 1)

    ones = jnp.ones((kw, LANES), BF16)

    def scores(u, slot):
        rr, r0, k0, tbl = place(u)
        for p in range(ATT_WIDTH // LANES):
            kp = kv_ref[0, rr, pl.ds(k0, kw), p * LANES:(p + 1) * LANES]
            q2 = q_ref[0, rr, pl.ds(r0, BLOCK), 2 * p * LANES:(2 * p + 2) * LANES]
            q2 = jnp.concatenate([q2[:, :LANES], q2[:, LANES:]], axis=0)
            bias2 = jnp.concatenate([bias_ref[tbl, 2 * p], bias_ref[tbl, 2 * p + 1]], axis=0)
            s = _dot_nt(q2, kp) + bias2
            m = jnp.max(s, axis=-1, keepdims=True)
            s_scr[slot, p] = jnp.exp(s - m).astype(BF16)
            m_scr[slot, p] = m

    def finish(u, slot):
        rr, r0, k0, _ = place(u)
        for p in range(ATT_WIDTH // LANES):
            vp = kv_ref[0, rr, pl.ds(k0, kw), ATT_WIDTH + p * LANES:ATT_WIDTH + (p + 1) * LANES]
            m = m_scr[slot, p]
            pv = _dot(s_scr[slot, p], jnp.concatenate([vp, ones], axis=1))
            l2 = pick(pv[:BLOCK, LANES:], pv[BLOCK:, LANES:])
            f1, f2 = _att_split(dil)
            if f1 == 1:
                rows = pl.ds(r0, BLOCK)
            else:
                rows = pl.ds((rr % f2) * (seq // f2) + r0 * f1 + rr // f2, BLOCK, stride=f1)
            o_ref[0, p, rows, :] = pick(pv[:BLOCK, :LANES], pv[BLOCK:, :LANES]) / l2
            lse_ref[0, p, rows, :] = pick(m[:BLOCK], m[BLOCK:]) + jnp.log(l2)

    scores(0, 0)

    def body(u, carry):
        slot = u % 2
        finish(u, slot)
        scores(u + 1, 1 - slot)
        return carry
    lax.fori_loop(0, n_units - 1, body, 0)
    finish(n_units - 1, (n_units - 1) % 2)


def _prompt_attention(qx, kvb, dil):
    bsz, _, n, _ = qx.shape
    assert n % BLOCK == 0
    nb = n // BLOCK
    bias = jnp.asarray(_prompt_bias(dil, min(nb, 2)))
    sub = lambda width: pl.BlockSpec((1, dil, n, width), lambda b: (b, 0, 0, 0))
    slabs = pl.BlockSpec((1, ATT_WIDTH // LANES, dil * n, LANES), lambda b: (b, 0, 0, 0))
    return pl.pallas_call(
        functools.partial(_attn_kernel, nb=nb, n_units=dil * nb, dil=dil),
        grid=(bsz,),
        in_specs=[sub(2 * ATT_WIDTH), sub(KV_WIDTH), _const_spec(bias.shape)],
        out_specs=[slabs, slabs],
        out_shape=[jax.ShapeDtypeStruct((bsz, ATT_WIDTH // LANES, dil * n, LANES), F32)] * 2,
        scratch_shapes=[pltpu.VMEM((2, N_SLOTS // 2, 2 * BLOCK, bias.shape[-1]), BF16),
                        pltpu.VMEM((2, N_SLOTS // 2, 2 * BLOCK, 1), F32)],
        compiler_params=_params(1),
        name=f"attn_d{dil}",
    )(qx, kvb, bias)


def _combine_groups(os_, lses):
    m = functools.reduce(jnp.maximum, lses)
    ws = [jnp.exp(l - m) for l in lses]
    den = functools.reduce(lambda a, b: a + b, ws)
    num = functools.reduce(lambda a, b: a + b, [w * o for w, o in zip(ws, os_)])
    return num / den


def _merge_to_ln1(x, m, ybuf, o_att, gate_c, gate_a,
                  wpc_ref, bpc_ref, wpa_ref, bpa_ref, wout_ref, bout_ref, g1_ref, b1_ref, alpha):
    y_conv = _dot(ybuf, wpc_ref[0]) + bpc_ref[0]
    y_att = _dot(o_att.astype(BF16), wpa_ref[0]) + bpa_ref[0]
    y = (gate_c * y_conv + gate_a * y_att).astype(BF16)
    y = _dot(y, wout_ref[0]) + bout_ref[0]
    return _layer_norm(alpha * x + (1.0 + _mod_slice(m, 2)) * y, g1_ref[0], b1_ref[0])


def _ffn_to_ln2(x1, m, wg_ref, wu_ref, wd_ref, g2_ref, b2_ref, alpha, chunk):
    u2 = (x1 * (1.0 + _mod_slice(m, 4)) + _mod_slice(m, 3)).astype(BF16)
    d_ff = wg_ref.shape[-1]
    h = None
    for c0 in range(0, d_ff, chunk):
        c1 = min(c0 + chunk, d_ff)
        a = (_silu(_dot(u2, wg_ref[0, :, c0:c1])) * _dot(u2, wu_ref[0, :, c0:c1])).astype(BF16)
        part = _dot(a, wd_ref[0, c0:c1, :])
        h = part if h is None else h + part
    return _layer_norm(alpha * x1 + (1.0 + _mod_slice(m, 5)) * h, g2_ref[0], b2_ref[0])


def _token_order(ref, scr_ref, slot, p):
    f2, n = ref.shape[2], ref.shape[3]
    if f2 == 1:
        return ref[0, p, 0]
    for b in range(f2):
        scr_ref[slot, pl.ds(b, n, stride=f2), :] = ref[0, p, b]
    return scr_ref[slot]


def _post_a_kernel(x_ref, mod_ref, ycv_ref, o0_ref, o1_ref, o2_ref, l0_ref, l1_ref, l2_ref,
                   gc_ref, ga_ref, wpc_ref, bpc_ref, wpa_ref, bpa_ref, wout_ref, bout_ref,
                   g1_ref, b1_ref, out_ref, scr_ref, *, alpha):
    o_refs = (o0_ref, o1_ref, o2_ref)
    l_refs = (l0_ref, l1_ref, l2_ref)
    chunks = []
    for p in range(ATT_WIDTH // LANES):
        os_ = [_token_order(o_refs[g], scr_ref, 2 * g, p) for g in range(N_GROUPS)]
        lses = [_token_order(l_refs[g], scr_ref, 2 * g + 1, p) for g in range(N_GROUPS)]
        chunks.append(_combine_groups(os_, lses).astype(BF16))
    o_att = jnp.concatenate(chunks, axis=1)
    out_ref[0] = _merge_to_ln1(x_ref[0], mod_ref[0], ycv_ref[0], o_att,
                               gc_ref[0].astype(F32), ga_ref[0].astype(F32),
                               wpc_ref, bpc_ref, wpa_ref, bpa_ref, wout_ref, bout_ref,
                               g1_ref, b1_ref, alpha)


def _post_a(x, mod, ycv, os_, lses, gate_c, gate_a, wts, layer, tm, alpha):
    bsz, s, d = x.shape
    nt = s // tm
    tile = lambda width: pl.BlockSpec((1, tm, width), lambda b, i: (b, i, 0))
    nslab = ATT_WIDTH // LANES
    splits = [_att_split(dil)[1] for dil in DILATIONS]
    slab_specs = [pl.BlockSpec((1, nslab, f2, tm // f2, LANES), lambda b, i: (b, 0, 0, i, 0)) for f2 in splits]
    view = lambda arrs: [a.reshape(bsz, nslab, f2, s // f2, LANES) for a, f2 in zip(arrs, splits)]
    return pl.pallas_call(
        functools.partial(_post_a_kernel, alpha=alpha),
        grid=(bsz, nt),
        in_specs=([tile(d), pl.BlockSpec((1, 1, MOD_WIDTH), lambda b, i: (b, 0, 0)), tile(CONV_CH)]
                  + slab_specs + slab_specs + [tile(d), tile(d)]
                  + [_layer_spec(w, layer) for w in wts]),
        out_specs=tile(d),
        out_shape=jax.ShapeDtypeStruct((bsz, s, d), F32),
        scratch_shapes=[pltpu.VMEM((2 * N_GROUPS, tm, LANES), F32)],
        compiler_params=_params(2),
        name="post_a",
    )(x, mod, ycv, *view(os_), *view(lses), gate_c, gate_a, *[_arg(w) for w in wts])


def _post_b_kernel(*refs, alpha, chunk, t_new, nsq, nblk):
    x_ref, mod_ref, wg_ref, wu_ref, wd_ref, g2_ref, b2_ref = refs[:7]
    q_refs, c_refs, n_refs = refs[7:10], refs[10:13], refs[13:16]
    bc_refs, bn_refs = refs[16:19], refs[19:22]
    out_ref, o_ref = refs[-5:-3]
    oc_refs = refs[-3:]
    out_ref[0] = _ffn_to_ln2(x_ref[0], mod_ref[0], wg_ref, wu_ref, wd_ref, g2_ref, b2_ref, alpha, chunk)
    step = pl.program_id(0) * pl.num_programs(1) + pl.program_id(1)
    first_seq = (step % nblk) * nsq
    for i in range(nsq):
        _sample_unit(q_refs, c_refs, n_refs, bc_refs, bn_refs, o_ref, oc_refs, i, first_seq + i, t_new)


def _post_b(x1, mod, wts, layer, tm, alpha, q8, caches_n, kvt_new, prev_outs, nseq, t_new):
    bsz, s, d = x1.shape
    nt = s // tm
    hsteps = ATT_WIDTH // LANES
    steps = bsz * nt
    assert (hsteps * nseq) % steps == 0 and nseq * t_new == LANES
    nsq = hsteps * nseq // steps
    nblk = nseq // nsq
    side = lambda f: (lambda b, i: f(*divmod(b * nt + i, nblk)))
    tables = [_sample_bias(g, caches_n[g].shape[-1], t_new) for g in range(N_GROUPS)]
    bc = [jnp.asarray(tc.reshape(hsteps, 2, Q_ROWS, -1)) for tc, _ in tables]
    bn = [jnp.asarray(tn.reshape(hsteps, 2, Q_ROWS, LANES)) for _, tn in tables]
    kvt4 = kvt_new.reshape(N_GROUPS, 2, ATT_WIDTH, nseq * t_new)
    tile = pl.BlockSpec((1, tm, d), lambda b, i: (b, i, 0))
    args = [x1, mod] + [_arg(w) for w in wts]
    in_specs = ([tile, pl.BlockSpec((1, 1, MOD_WIDTH), lambda b, i: (b, 0, 0))]
                + [_layer_spec(w, layer) for w in wts])
    for g in range(N_GROUPS):
        in_specs.append(pl.BlockSpec((nsq, Q_ROWS, LANES), side(lambda hc, blk, g=g: (blk, 0, g * hsteps + hc))))
        args.append(q8)
    cache_specs = [pl.BlockSpec((nsq, 2, LANES, c.shape[-1]),
                                side(lambda hc, blk: (layer * nblk + blk, 0, hc, 0))) for c in caches_n]
    in_specs += cache_specs
    args += list(caches_n)
    for g in range(N_GROUPS):
        in_specs.append(pl.BlockSpec((1, 2, LANES, LANES), side(lambda hc, blk, g=g: (g, 0, hc, 0))))
        args.append(kvt4)
    for tbl in bc + bn:
        in_specs.append(pl.BlockSpec((1,) + tbl.shape[1:], side(lambda hc, blk: (hc, 0, 0, 0))))
        args.append(tbl)
    aliases = {}
    if prev_outs is not None:
        for g in range(N_GROUPS):
            in_specs.append(pl.BlockSpec(memory_space=pl.ANY))
            aliases[len(args)] = 2 + g
            args.append(prev_outs[g])
    outs = pl.pallas_call(
        functools.partial(_post_b_kernel, alpha=alpha, chunk=1024, t_new=t_new, nsq=nsq, nblk=nblk),
        grid=(bsz, nt),
        in_specs=in_specs,
        out_specs=[tile, pl.BlockSpec((nsq, Q_ROWS, LANES), side(lambda hc, blk: (blk, 0, hc)))] + cache_specs,
        out_shape=[jax.ShapeDtypeStruct((bsz, s, d), F32),
                   jax.ShapeDtypeStruct((nseq, Q_ROWS, ATT_WIDTH), F32)]
                  + [jax.ShapeDtypeStruct(c.shape, F32) for c in caches_n],
        input_output_aliases=aliases,
        compiler_params=_params(2),
        name="post_b",
    )(*args)
    return outs[0], outs[1], outs[2:]


def _post_s_kernel(x_ref, mod_ref, state_ref, glu_ref, o_ref, gc_ref, ga_ref,
                   wdw_ref, bdw_ref, cg_ref, cb_ref,
                   wpc_ref, bpc_ref, wpa_ref, bpa_ref, wout_ref, bout_ref, g1_ref, b1_ref,
                   wg_ref, wu_ref, wd_ref, g2_ref, b2_ref,
                   out_ref, nconv_ref, ypre_ref, *, nseq, t_new, alpha, chunk):
    ctx = CONV_WIDTH - 1

    def slab(i):
        if i < ctx:
            return state_ref[0, i]
        return glu_ref[0, (i - ctx) * nseq:(i - ctx + 1) * nseq, :]

    for t in range(t_new):
        acc = jnp.zeros((nseq, CONV_CH), F32) + bdw_ref[0]
        for j in range(CONV_WIDTH):
            acc = acc + slab(t + j) * wdw_ref[0, j:j + 1, :]
        ypre_ref[t * nseq:(t + 1) * nseq, :] = acc
    for i in range(ctx):
        nconv_ref[0, i] = slab(i + t_new)
    ybuf = _silu(_layer_norm(ypre_ref[...], cg_ref[0], cb_ref[0])).astype(BF16)
    m = mod_ref[0]
    x1 = _merge_to_ln1(x_ref[0], m, ybuf, o_ref[0], gc_ref[0].astype(F32), ga_ref[0].astype(F32),
                       wpc_ref, bpc_ref, wpa_ref, bpa_ref, wout_ref, bout_ref, g1_ref, b1_ref, alpha)
    out_ref[0] = _ffn_to_ln2(x1, m, wg_ref, wu_ref, wd_ref, g2_ref, b2_ref, alpha, chunk)


def _post_s(x, mod, state_n, layer, glu, o_att, gate_c, gate_a, wts, nseq, t_new, alpha):
    _, m, d = x.shape
    ctx = CONV_WIDTH - 1
    full = lambda *shape: _const_spec(shape)
    whole = lambda *shape: pl.BlockSpec(shape, lambda i: (0,) * len(shape))
    return pl.pallas_call(
        functools.partial(_post_s_kernel, nseq=nseq, t_new=t_new, alpha=alpha, chunk=1024),
        in_specs=[full(1, m, d), full(1, m, MOD_WIDTH), _layer_spec(state_n, layer), full(1, m, CONV_CH),
                  full(1, m, ATT_WIDTH), full(1, m, d), full(1, m, d)]
                 + [_layer_spec(w, layer) for w in wts],
        out_specs=[whole(1, m, d), whole(1, ctx, nseq, CONV_CH)],
        out_shape=[jax.ShapeDtypeStruct((1, m, d), F32),
                   jax.ShapeDtypeStruct((1, ctx, nseq, CONV_CH), F32)],
        scratch_shapes=[pltpu.VMEM((m, CONV_CH), F32)],
        grid=(1,),
        compiler_params=_params(1),
        name="post_s",
    )(x, mod, state_n, glu, o_att, gate_c, gate_a, *[_arg(w) for w in wts])


def _sample_bias(g, length, t_new):
    dil = DILATIONS[g]
    slopes = _alibi_slopes()[:, None, None]
    t = np.arange(Q_ROWS)[:, None]
    p = np.arange(length)[None, :]
    dist = length + t - p
    valid = (dist % dil == 0) & (dist // dil >= 1) & (dist // dil <= SPAN) & (t < t_new)
    cache = np.where(valid[None], -slopes * dist.astype(np.float32)[None], np.float32(NEG))
    tp = np.arange(LANES)[None, :] - (LANES - t_new)
    dist_n = t - tp
    valid_n = (tp >= 0) & (dist_n >= 0) & (dist_n % dil == 0) & (dist_n // dil <= SPAN) & (t < t_new)
    new = np.where(valid_n[None], -slopes * dist_n.astype(np.float32)[None], np.float32(NEG))
    return cache.astype(np.float32), new.astype(np.float32)


def _sample_unit(q_refs, c_refs, n_refs, bc_refs, bn_refs, o_ref, oc_refs, i, seq, t_new):
    shift = (LANES - t_new) - t_new * seq
    lane = lax.broadcasted_iota(jnp.int32, (LANES, LANES), 1)
    heads_per_step = LANES // HEAD_DIM
    per_head = [[] for _ in range(heads_per_step)]
    for g in range(N_GROUPS):
        c_ref = c_refs[g]
        length = c_ref.shape[-1]
        new = [pltpu.roll(n_refs[g][0, kv], shift, axis=1) for kv in range(2)]
        for e in range(heads_per_step):
            rows = slice(e * HEAD_DIM, (e + 1) * HEAD_DIM)
            qh = q_refs[g][i, :, rows].astype(BF16)
            s_c = _dot(qh, c_ref[i, 0, rows, :].astype(BF16)) + bc_refs[g][0, e]
            s_n = _dot(qh, new[0][rows, :].astype(BF16)) + bn_refs[g][0, e]
            m = jnp.maximum(jnp.max(s_c, axis=-1, keepdims=True), jnp.max(s_n, axis=-1, keepdims=True))
            p_c = jnp.exp(s_c - m)
            p_n = jnp.exp(s_n - m)
            l = jnp.sum(p_c, axis=-1, keepdims=True) + jnp.sum(p_n, axis=-1, keepdims=True)
            o = (_dot_nt(p_c.astype(BF16), c_ref[i, 1, rows, :].astype(BF16))
                 + _dot_nt(p_n.astype(BF16), new[1][rows, :].astype(BF16)))
            per_head[e].append((o / l, m + jnp.log(l)))
        for kv in range(2):
            rolled = pltpu.roll(c_ref[i, kv], length - t_new, axis=1)
            if length > LANES:
                oc_refs[g][i, kv, :, 0:length - LANES] = rolled[:, 0:length - LANES]
            oc_refs[g][i, kv, :, length - LANES:length] = jnp.where(
                lane >= LANES - t_new, new[kv], rolled[:, length - LANES:length])
    for e in range(heads_per_step):
        os_, lses = zip(*per_head[e])
        o_ref[i, :, e * HEAD_DIM:(e + 1) * HEAD_DIM] = _combine_groups(list(os_), list(lses))


def _native_view(cache):
    depth, nseq, length = cache.shape[:3]
    return cache.transpose(0, 1, 3, 4, 5, 2).reshape(depth * nseq, 2, ATT_WIDTH, length)


def _from_native(x, depth, nseq):
    length = x.shape[-1]
    return x.reshape(depth, nseq, 2, N_SLOTS, HEAD_DIM, length).transpose(0, 1, 5, 2, 3, 4)


def kernel(x_prompt, x_sample, c_prompt, c_sample, cache_kv_g0, cache_kv_g1, cache_kv_g2, state_conv, w_ada, b_ada, w_in, b_in, w_dw, b_dw, conv_ln_g, conv_ln_b, w_pc, b_pc, w_pa, b_pa, w_out, b_out, ln1_g, ln1_b, w_gate, w_up, w_down, ln2_g, ln2_b):
    depth = w_in.shape[0]
    bsz, seq, d = x_prompt.shape
    nseq, t_new, _ = x_sample.shape
    alpha = (2 * depth) ** 0.25
    tm = 512
    tm_a = 1024
    tm_ffn = 256
    m_s = nseq * t_new

    mod_all = _adaln(jnp.concatenate([c_prompt, c_sample], axis=0), w_ada, b_ada)
    caches_n = [_native_view(c) for c in (cache_kv_g0, cache_kv_g1, cache_kv_g2)]
    state_n = state_conv.transpose(0, 2, 1, 3)

    w_in_bf = w_in.astype(BF16)
    (b_in3, v_bdw, v_clg, v_clb, v_bpc, v_bpa, v_bout, v_l1g, v_l1b, v_l2g, v_l2b) = _pack_vectors(
        [b_in, b_dw, conv_ln_g, conv_ln_b, b_pc, b_pa, b_out, ln1_g, ln1_b, ln2_g, ln2_b])
    conv_w = [w_dw, v_bdw, v_clg, v_clb]
    merge_w = [w_pc.astype(BF16), v_bpc, w_pa.astype(BF16), v_bpa, w_out.astype(BF16), v_bout, v_l1g, v_l1b]
    ffn_w = [w_gate.astype(BF16), w_up.astype(BF16), w_down.astype(BF16), v_l2g, v_l2b]

    xp = x_prompt
    xs = x_sample.transpose(1, 0, 2).reshape(1, m_s, d)
    conv_prompt, conv_sample = [], []
    cache_outs = None
    kv_nat = None
    for l in range(depth):
        mod_p = mod_all[l, :bsz].reshape(bsz, 1, MOD_WIDTH)
        mod_s = jnp.tile(mod_all[l, bsz:], (t_new, 1)).reshape(1, m_s, MOD_WIDTH)
        mod_sq = jnp.repeat(mod_all[l, bsz:], t_new, axis=0).reshape(1, m_s, MOD_WIDTH)

        xs_q = xs.reshape(t_new, nseq, d).transpose(1, 0, 2).reshape(1, m_s, d)
        glu_s, gc_s, ga_s, q_s, kvt_new = _inproj_s(xs, mod_s, xs_q, mod_sq, w_in_bf, b_in3, l)
        q8 = jnp.pad(q_s.reshape(t_new, nseq, N_GROUPS * ATT_WIDTH).transpose(1, 0, 2),
                     ((0, 0), (0, Q_ROWS - t_new), (0, 0)))

        glu, gate_c, gate_a, *qx = _inproj_a(xp, mod_p, w_in_bf, b_in3, l, tm_a)
        kvb, kv_nat, ycv = _inproj_b(xp, mod_p, w_in_bf, b_in3, glu, conv_w, l, depth, kv_nat, tm)
        conv_prompt.append(glu[:, seq - (CONV_WIDTH - 1):, :])
        os_, lses = [], []
        for g in range(N_GROUPS):
            o, lse = _prompt_attention(qx[g], kvb[g], DILATIONS[g])
            os_.append(o)
            lses.append(lse)
        x1 = _post_a(xp, mod_p, ycv, os_, lses, gate_c, gate_a, merge_w, l, tm, alpha)
        xp, o8, cache_outs = _post_b(x1, mod_p, ffn_w, l, tm_ffn, alpha,
                                     q8, caches_n, kvt_new, cache_outs, nseq, t_new)

        o_s = o8[:, :t_new].transpose(1, 0, 2).reshape(1, m_s, ATT_WIDTH)
        xs, nconv = _post_s(xs, mod_s, state_n, l, glu_s, o_s, gc_s, ga_s, conv_w + merge_w + ffn_w,
                            nseq, t_new, alpha)
        conv_sample.append(nconv[0])

    kv_p = [_from_native(kv_nat[g].reshape(depth * bsz, 2, ATT_WIDTH, -1), depth, bsz)
            for g in range(N_GROUPS)]
    kv_s = [_from_native(cache_outs[g], depth, nseq) for g in range(N_GROUPS)]
    return (xp, xs.reshape(t_new, nseq, d).transpose(1, 0, 2),
            kv_p[0], kv_p[1], kv_p[2], jnp.stack(conv_prompt),
            kv_s[0], kv_s[1], kv_s[2], jnp.stack(conv_sample).transpose(0, 2, 1, 3))
```

```python
import functools

import numpy as np
import jax
import jax.numpy as jnp
from jax import lax
from jax.experimental import pallas as pl
from jax.experimental.pallas import tpu as pltpu

D_MODEL = 1024
CONV_CH = 512
CONV_WIDTH = 31
N_SLOTS = 8
HEAD_DIM = 64
ATT_WIDTH = N_SLOTS * HEAD_DIM
WINDOWS = (128, 512, 2048)
DILATIONS = (1, 4, 16)
N_GROUPS = 3
SPAN = 128
BLOCK = 128
LN_EPS = 1e-5
QKV_OFF = 2 * CONV_CH
GATE_OFF = QKV_OFF + 3 * N_GROUPS * ATT_WIDTH
IN_WIDTH = GATE_OFF + 2 * D_MODEL
KV_WIDTH = 2 * ATT_WIDTH
MOD_WIDTH = 6 * D_MODEL
LANES = 128
NEG = -1e30
Q_ROWS = 8
CONV_HALO = 32
VMEM_LIMIT = 56 * 1024 * 1024

F32 = jnp.float32
BF16 = jnp.bfloat16
GATE_DTYPE = jnp.bfloat16

_NT = (((1,), (1,)), ((), ()))


def _q_col(g):
    return QKV_OFF + g * ATT_WIDTH


def _k_col(g):
    return QKV_OFF + (N_GROUPS + g) * ATT_WIDTH


def _v_col(g):
    return QKV_OFF + (2 * N_GROUPS + g) * ATT_WIDTH


def _dot(a, b):
    return jnp.dot(a, b, preferred_element_type=F32)


def _dot_nt(a, b):
    return lax.dot_general(a, b, _NT, preferred_element_type=F32)


def _sigmoid(x):
    return 1.0 / (1.0 + jnp.exp(-x))


def _silu(x):
    return x * _sigmoid(x)


def _layer_norm(x, g, b):
    mu = jnp.mean(x, axis=-1, keepdims=True)
    xc = x - mu
    var = jnp.mean(xc * xc, axis=-1, keepdims=True)
    return xc * lax.rsqrt(var + LN_EPS) * g + b


def _mod_slice(m, i):
    return m[:, i * D_MODEL:(i + 1) * D_MODEL]


def _const_spec(shape):
    nd = len(shape)
    return pl.BlockSpec(shape, lambda *_: (0,) * nd, pipeline_mode=pl.Buffered(1))


def _layer_spec(arr, layer):
    nd = arr.ndim
    return pl.BlockSpec((1,) + arr.shape[1:], lambda *_: (layer,) + (0,) * (nd - 1),
                        pipeline_mode=pl.Buffered(1))


def _params(n_axes):
    return pltpu.CompilerParams(dimension_semantics=("arbitrary",) * n_axes,
                                vmem_limit_bytes=VMEM_LIMIT)


def _adaln_kernel(c_ref, w_ref, b_ref, o_ref):
    c = c_ref[...]
    s = _silu(c).astype(BF16)
    o_ref[0] = _dot(s, w_ref[0].astype(BF16)) + b_ref[0]


def _adaln(c_all, w_ada, b_ada):
    depth, d, width = w_ada.shape
    rows = c_all.shape[0]
    tn = 1024
    return pl.pallas_call(
        _adaln_kernel,
        grid=(depth, width // tn),
        in_specs=[pl.BlockSpec((rows, d), lambda l, j: (0, 0)),
                  pl.BlockSpec((1, d, tn), lambda l, j: (l, 0, j)),
                  pl.BlockSpec((1, 1, tn), lambda l, j: (l, 0, j))],
        out_specs=pl.BlockSpec((1, rows, tn), lambda l, j: (l, 0, j)),
        out_shape=jax.ShapeDtypeStruct((depth, rows, width), F32),
        compiler_params=_params(2),
        name="adaln",
    )(c_all, w_ada, b_ada.reshape(depth, 1, width))


def _modulated(x_ref, mod_ref, shift_i, scale_i):
    m = mod_ref[0]
    x = x_ref[0]
    return (x * (1.0 + _mod_slice(m, scale_i)) + _mod_slice(m, shift_i)).astype(BF16)


def _proj(u, w_ref, b_ref, c0, width):
    return _dot(u, w_ref[0, :, c0:c0 + width]) + b_ref[0, :, c0:c0 + width]


def _glu_gates_q(u, w_ref, b_ref):
    glu = _proj(u, w_ref, b_ref, 0, CONV_CH) * _sigmoid(_proj(u, w_ref, b_ref, CONV_CH, CONV_CH))
    gate_c = _sigmoid(_proj(u, w_ref, b_ref, GATE_OFF, D_MODEL))
    gate_a = _sigmoid(_proj(u, w_ref, b_ref, GATE_OFF + D_MODEL, D_MODEL))
    qs = [_proj(u, w_ref, b_ref, _q_col(g), ATT_WIDTH) * (HEAD_DIM ** -0.5) for g in range(N_GROUPS)]
    return glu, gate_c, gate_a, qs


def _residue_rows(t, scr_ref, slot, dil):
    if dil == 1:
        return [t]
    rows = t.shape[0]
    scr_ref[slot, 0:rows, :] = t
    if dil <= 4:
        return [scr_ref[slot, pl.ds(r, rows // dil, stride=dil), :] for r in range(dil)]
    assert dil == 16
    quarter = rows // 4
    for b in range(4):
        scr_ref[slot, rows + b * quarter:rows + (b + 1) * quarter, :] = scr_ref[slot, pl.ds(b, quarter, stride=4), :]
    return [scr_ref[slot, pl.ds(rows + (r % 4) * quarter + r // 4, rows // dil, stride=4), :] for r in range(dil)]


def _inproj_a_kernel(x_ref, mod_ref, w_ref, b_ref, glu_ref, gc_ref, ga_ref, q0_ref, q1_ref, q2_ref,
                     scr_ref):
    u = _modulated(x_ref, mod_ref, 0, 1)
    glu, gate_c, gate_a, qs = _glu_gates_q(u, w_ref, b_ref)
    glu_ref[0] = glu
    gc_ref[0] = gate_c.astype(gc_ref.dtype)
    ga_ref[0] = gate_a.astype(ga_ref.dtype)
    for g, (q, q_ref) in enumerate(zip(qs, (q0_ref, q1_ref, q2_ref))):
        dil = DILATIONS[g]
        for p in range(ATT_WIDTH // LANES):
            parts = _residue_rows(q[:, p * LANES:(p + 1) * LANES], scr_ref, p, dil)
            for r, t in enumerate(parts):
                lane = lax.broadcasted_iota(jnp.int32, t.shape, 1)
                q_ref[0, r, :, (2 * p) * LANES:(2 * p + 1) * LANES] = jnp.where(lane < HEAD_DIM, t, 0.0).astype(BF16)
                q_ref[0, r, :, (2 * p + 1) * LANES:(2 * p + 2) * LANES] = jnp.where(lane >= HEAD_DIM, t, 0.0).astype(BF16)


def _residue_spec(tm, dil, width):
    return pl.BlockSpec((1, dil, tm // dil, width), lambda b, i: (b, 0, i, 0))


def _inproj_a(x, mod, w_in, b_in, layer, tm):
    bsz, s, d = x.shape
    nt = s // tm
    tile = lambda width: pl.BlockSpec((1, tm, width), lambda b, i: (b, i, 0))
    out_shape = ([jax.ShapeDtypeStruct((bsz, s, CONV_CH), F32),
                  jax.ShapeDtypeStruct((bsz, s, d), GATE_DTYPE),
                  jax.ShapeDtypeStruct((bsz, s, d), GATE_DTYPE)]
                 + [jax.ShapeDtypeStruct((bsz, dil, s // dil, 2 * ATT_WIDTH), BF16) for dil in DILATIONS])
    return pl.pallas_call(
        _inproj_a_kernel,
        grid=(bsz, nt),
        in_specs=[tile(d),
                  pl.BlockSpec((1, 1, MOD_WIDTH), lambda b, i: (b, 0, 0)),
                  _layer_spec(w_in, layer), _layer_spec(b_in, layer)],
        out_specs=[tile(CONV_CH), tile(d), tile(d)]
                  + [_residue_spec(tm, dil, 2 * ATT_WIDTH) for dil in DILATIONS],
        out_shape=out_shape,
        scratch_shapes=[pltpu.VMEM((ATT_WIDTH // LANES, 2 * tm, LANES), F32)],
        compiler_params=_params(2),
        name="inproj_a",
    )(x, mod, w_in, b_in)


def _conv_ln_silu(win_ref, r0, rb, wdw_ref, bdw_ref, cg_ref, cb_ref, out_ref):
    off = CONV_HALO - (CONV_WIDTH - 1)
    accs = []
    for c in range(CONV_CH // LANES):
        cols = slice(c * LANES, (c + 1) * LANES)
        acc = jnp.zeros((rb, LANES), F32) + bdw_ref[0, :, cols]
        for j in range(CONV_WIDTH):
            acc = acc + win_ref[c, pl.ds(r0 + off + j, rb, stride=1), :] * wdw_ref[0, j:j + 1, cols]
        accs.append(acc)
    y = _silu(_layer_norm(jnp.concatenate(accs, axis=1), cg_ref[0], cb_ref[0]))
    out_ref[0, r0:r0 + rb, :] = y.astype(out_ref.dtype)


def _inproj_b_kernel(*refs, tm, seq, aliased):
    (x_ref, mod_ref, w_ref, b_ref, glu_ref, halo_ref,
     wdw_ref, bdw_ref, cg_ref, cb_ref) = refs[:10]
    n_in = 10 + (N_GROUPS if aliased else 0)
    kv_refs = refs[n_in:n_in + N_GROUPS]
    nat_refs = refs[n_in + N_GROUPS:n_in + 2 * N_GROUPS]
    ycv_ref, scr_ref, win_ref, keep_ref = refs[n_in + 2 * N_GROUPS:]
    ti = pl.program_id(1)
    halo = halo_ref[0]
    halo = jnp.where(ti > 0, halo, jnp.zeros_like(halo))
    glu = glu_ref[0]
    for c in range(CONV_CH // LANES):
        win_ref[c, 0:CONV_HALO, :] = halo[:, c * LANES:(c + 1) * LANES]
        win_ref[c, CONV_HALO:CONV_HALO + tm, :] = glu[:, c * LANES:(c + 1) * LANES]
    u = _modulated(x_ref, mod_ref, 0, 1)
    wins = [min(w, seq) for w in WINDOWS]
    every_tile = [w >= seq for w in wins]

    rb = 64
    for r0 in range(0, tm, rb):
        _conv_ln_silu(win_ref, r0, rb, wdw_ref, bdw_ref, cg_ref, cb_ref, ycv_ref)
    for g, kv_ref in enumerate(kv_refs):
        for half, c0 in enumerate((_k_col(g), _v_col(g))):
            t = _proj(u, w_ref, b_ref, c0, ATT_WIDTH)
            for p in range(ATT_WIDTH // LANES):
                parts = _residue_rows(t[:, p * LANES:(p + 1) * LANES], scr_ref, half * 4 + p, DILATIONS[g])
                for r, part in enumerate(parts):
                    cols = slice(half * ATT_WIDTH + p * LANES, half * ATT_WIDTH + (p + 1) * LANES)
                    kv_ref[0, r, :, cols] = part.astype(BF16)
            if every_tile[g]:
                nat_refs[g][0, half * ATT_WIDTH:(half + 1) * ATT_WIDTH, :] = t.T
            else:
                keep_ref[2 * g + half] = t
    for g, n_ref in enumerate(nat_refs):
        if every_tile[g]:
            continue
        win = wins[g]
        rows = min(win, tm)
        cond = (ti >= (seq - win) // tm) if win >= tm else (ti == seq // tm - 1)

        @pl.when(cond)
        def _(g=g, n_ref=n_ref, rows=rows):
            for half in range(2):
                n_ref[0, half * ATT_WIDTH:(half + 1) * ATT_WIDTH, :] = keep_ref[2 * g + half, tm - rows:tm, :].T


def _inproj_b(x, mod, w_in, b_in, glu, conv_w, layer, depth, prev_nat, tm):
    bsz, s, d = x.shape
    nt = s // tm
    assert tm % CONV_HALO == 0
    tile = lambda width: pl.BlockSpec((1, tm, width), lambda b, i: (b, i, 0))
    halo_spec = pl.BlockSpec(
        (1, CONV_HALO, CONV_CH), lambda b, i: (b, jnp.maximum(i * (tm // CONV_HALO) - 1, 0), 0))
    nat_specs, nat_shapes = [], []
    for g in range(N_GROUPS):
        win = min(WINDOWS[g], s)
        assert win % tm == 0 or tm % win == 0
        nat_shapes.append(jax.ShapeDtypeStruct((depth * bsz, KV_WIDTH, win), F32))
        if win >= tm:
            first = (s - win) // tm
            nat_specs.append(pl.BlockSpec(
                (1, KV_WIDTH, tm),
                lambda b, i, first=first: (layer * bsz + b, 0, jnp.maximum(i - first, 0))))
        else:
            nat_specs.append(pl.BlockSpec((1, KV_WIDTH, win), lambda b, i: (layer * bsz + b, 0, 0)))
    args = [x, mod, w_in, b_in, glu, glu, *conv_w]
    in_specs = [tile(d), pl.BlockSpec((1, 1, MOD_WIDTH), lambda b, i: (b, 0, 0)),
                _layer_spec(w_in, layer), _layer_spec(b_in, layer),
                tile(CONV_CH), halo_spec] + [_layer_spec(w, layer) for w in conv_w]
    aliases = {}
    if prev_nat is not None:
        for g in range(N_GROUPS):
            in_specs.append(pl.BlockSpec(memory_space=pl.ANY))
            aliases[len(args)] = N_GROUPS + g
            args.append(prev_nat[g])
    outs = pl.pallas_call(
        functools.partial(_inproj_b_kernel, tm=tm, seq=s, aliased=prev_nat is not None),
        grid=(bsz, nt),
        in_specs=in_specs,
        out_specs=[_residue_spec(tm, dil, KV_WIDTH) for dil in DILATIONS] + nat_specs + [tile(CONV_CH)],
        out_shape=[jax.ShapeDtypeStruct((bsz, dil, s // dil, KV_WIDTH), BF16) for dil in DILATIONS]
                  + nat_shapes + [jax.ShapeDtypeStruct((bsz, s, CONV_CH), BF16)],
        scratch_shapes=[pltpu.VMEM((KV_WIDTH // LANES, 2 * tm, LANES), F32),
                        pltpu.VMEM((CONV_CH // LANES, CONV_HALO + tm, LANES), F32),
                        pltpu.VMEM((2 * N_GROUPS, tm, ATT_WIDTH), F32)],
        input_output_aliases=aliases,
        compiler_params=_params(2),
        name="inproj_b",
    )(*args)
    return outs[:N_GROUPS], outs[N_GROUPS:2 * N_GROUPS], outs[2 * N_GROUPS]


def _inproj_s_kernel(x_ref, mod_ref, xq_ref, modq_ref, w_ref, b_ref,
                     glu_ref, gc_ref, ga_ref, q_ref, kvt_ref):
    u = _modulated(x_ref, mod_ref, 0, 1)
    glu, gate_c, gate_a, qs = _glu_gates_q(u, w_ref, b_ref)
    glu_ref[0] = glu
    gc_ref[0] = gate_c.astype(gc_ref.dtype)
    ga_ref[0] = gate_a.astype(ga_ref.dtype)
    for g, q in enumerate(qs):
        q_ref[0, :, g * ATT_WIDTH:(g + 1) * ATT_WIDTH] = q
    uq = _modulated(xq_ref, modq_ref, 0, 1)
    for g in range(N_GROUPS):
        for half, c0 in enumerate((_k_col(g), _v_col(g))):
            r0 = g * KV_WIDTH + half * ATT_WIDTH
            kvt_ref[r0:r0 + ATT_WIDTH, :] = _proj(uq, w_ref, b_ref, c0, ATT_WIDTH).T


def _inproj_s(x, mod, xq, modq, w_in, b_in, layer):
    _, m, d = x.shape
    full = lambda *shape: _const_spec(shape)
    whole = lambda *shape: pl.BlockSpec(shape, lambda i: (0,) * len(shape))
    return pl.pallas_call(
        _inproj_s_kernel,
        in_specs=[full(1, m, d), full(1, m, MOD_WIDTH), full(1, m, d), full(1, m, MOD_WIDTH),
                  _layer_spec(w_in, layer), _layer_spec(b_in, layer)],
        out_specs=[whole(1, m, CONV_CH), whole(1, m, d), whole(1, m, d),
                   whole(1, m, N_GROUPS * ATT_WIDTH), whole(N_GROUPS * KV_WIDTH, m)],
        out_shape=[jax.ShapeDtypeStruct((1, m, CONV_CH), F32),
                   jax.ShapeDtypeStruct((1, m, d), GATE_DTYPE),
                   jax.ShapeDtypeStruct((1, m, d), GATE_DTYPE),
                   jax.ShapeDtypeStruct((1, m, N_GROUPS * ATT_WIDTH), F32),
                   jax.ShapeDtypeStruct((N_GROUPS * KV_WIDTH, m), F32)],
        grid=(1,),
        compiler_params=_params(1),
        name="inproj_s",
    )(x, mod, xq, modq, w_in, b_in)


def _alibi_slopes():
    return (2.0 ** (-8.0 * (np.arange(N_SLOTS) + 1) / N_SLOTS)).astype(np.float32)


def _prompt_bias(dil, key_blocks):
    qi = np.arange(BLOCK)[:, None]
    kj = np.arange(key_blocks * BLOCK)[None, :]
    tables = []
    for lead in (0, key_blocks - 1):
        rel = lead * BLOCK + qi - kj
        valid = (rel >= 0) & (rel <= SPAN)
        bias = -_alibi_slopes()[:, None, None] * (rel * dil).astype(np.float32)[None]
        tables.append(np.where(valid[None], bias, np.float32(NEG)))
    return np.stack(tables).astype(np.float32)


def _att_split(dil):
    f1 = min(dil, 4)
    assert dil % f1 == 0
    return f1, dil // f1


def _attn_kernel(q_ref, kv_ref, bias_ref, o_ref, lse_ref, s_scr, m_scr, *, nb, n_units, dil):
    kw = bias_ref.shape[-1]
    seq = o_ref.shape[2]
    lane = lax.broadcasted_iota(jnp.int32, (BLOCK, LANES), 1)
    pick = lambda a, b: jnp.where(lane < HEAD_DIM, a, b)

    def place(u):
        if isinstance(u, int):
            rr, c = divmod(u, nb)
            r0 = c * BLOCK
            k0 = max(r0 - (kw - BLOCK), 0)
            return rr, r0, k0, (0 if c == 0 else 1)
        rr = u // nb
        c = u % nb
        r0 = pl.multiple_of(c * BLOCK, BLOCK)
        k0 = pl.multiple_of(jnp.maximum(r0 - (kw - BLOCK), 0), BLOCK)
        return rr, r0, k0, jnp.minimum(c, 1)

    ones = jnp.ones((kw, LANES), BF16)

    def scores(u, slot):
        rr, r0, k0, tbl = place(u)
        for p in range(ATT_WIDTH // LANES):
            kp = kv_ref[0, rr, pl.ds(k0, kw), p * LANES:(p + 1) * LANES]
            q2 = q_ref[0, rr, pl.ds(r0, BLOCK), 2 * p * LANES:(2 * p + 2) * LANES]
            q2 = jnp.concatenate([q2[:, :LANES], q2[:, LANES:]], axis=0)
            bias2 = jnp.concatenate([bias_ref[tbl, 2 * p], bias_ref[tbl, 2 * p + 1]], axis=0)
            s = _dot_nt(q2, kp) + bias2
            m = jnp.max(s, axis=-1, keepdims=True)
            s_scr[slot, p] = jnp.exp(s - m).astype(BF16)
            m_scr[slot, p] = m

    def finish(u, slot):
        rr, r0, k0, _ = place(u)
        for p in range(ATT_WIDTH // LANES):
            vp = kv_ref[0, rr, pl.ds(k0, kw), ATT_WIDTH + p * LANES:ATT_WIDTH + (p + 1) * LANES]
            m = m_scr[slot, p]
            pv = _dot(s_scr[slot, p], jnp.concatenate([vp, ones], axis=1))
            l2 = pick(pv[:BLOCK, LANES:], pv[BLOCK:, LANES:])
            f1, f2 = _att_split(dil)
            if f1 == 1:
                rows = pl.ds(r0, BLOCK)
            else:
                rows = pl.ds((rr % f2) * (seq // f2) + r0 * f1 + rr // f2, BLOCK, stride=f1)
            o_ref[0, p, rows, :] = pick(pv[:BLOCK, :LANES], pv[BLOCK:, :LANES]) / l2
            lse_ref[0, p, rows, :] = pick(m[:BLOCK], m[BLOCK:]) + jnp.log(l2)

    scores(0, 0)

    def body(u, carry):
        slot = u % 2
        finish(u, slot)
        scores(u + 1, 1 - slot)
        return carry
    lax.fori_loop(0, n_units - 1, body, 0)
    finish(n_units - 1, (n_units - 1) % 2)


def _prompt_attention(qx, kvb, dil):
    bsz, _, n, _ = qx.shape
    assert n % BLOCK == 0
    nb = n // BLOCK
    bias = jnp.asarray(_prompt_bias(dil, min(nb, 2)))
    sub = lambda width: pl.BlockSpec((1, dil, n, width), lambda b: (b, 0, 0, 0))
    slabs = pl.BlockSpec((1, ATT_WIDTH // LANES, dil * n, LANES), lambda b: (b, 0, 0, 0))
    return pl.pallas_call(
        functools.partial(_attn_kernel, nb=nb, n_units=dil * nb, dil=dil),
        grid=(bsz,),
        in_specs=[sub(2 * ATT_WIDTH), sub(KV_WIDTH), _const_spec(bias.shape)],
        out_specs=[slabs, slabs],
        out_shape=[jax.ShapeDtypeStruct((bsz, ATT_WIDTH // LANES, dil * n, LANES), F32)] * 2,
        scratch_shapes=[pltpu.VMEM((2, N_SLOTS // 2, 2 * BLOCK, bias.shape[-1]), BF16),
                        pltpu.VMEM((2, N_SLOTS // 2, 2 * BLOCK, 1), F32)],
        compiler_params=_params(1),
        name=f"attn_d{dil}",
    )(qx, kvb, bias)


def _combine_groups(os_, lses):
    m = functools.reduce(jnp.maximum, lses)
    ws = [jnp.exp(l - m) for l in lses]
    den = functools.reduce(lambda a, b: a + b, ws)
    num = functools.reduce(lambda a, b: a + b, [w * o for w, o in zip(ws, os_)])
    return num / den


def _merge_to_ln1(x, m, ybuf, o_att, gate_c, gate_a,
                  wpc_ref, bpc_ref, wpa_ref, bpa_ref, wout_ref, bout_ref, g1_ref, b1_ref, alpha):
    y_conv = _dot(ybuf, wpc_ref[0]) + bpc_ref[0]
    y_att = _dot(o_att.astype(BF16), wpa_ref[0]) + bpa_ref[0]
    y = (gate_c * y_conv + gate_a * y_att).astype(BF16)
    y = _dot(y, wout_ref[0]) + bout_ref[0]
    return _layer_norm(alpha * x + (1.0 + _mod_slice(m, 2)) * y, g1_ref[0], b1_ref[0])


def _ffn_to_ln2(x1, m, wg_ref, wu_ref, wd_ref, g2_ref, b2_ref, alpha, chunk):
    u2 = (x1 * (1.0 + _mod_slice(m, 4)) + _mod_slice(m, 3)).astype(BF16)
    d_ff = wg_ref.shape[-1]
    h = None
    for c0 in range(0, d_ff, chunk):
        c1 = min(c0 + chunk, d_ff)
        a = (_silu(_dot(u2, wg_ref[0, :, c0:c1])) * _dot(u2, wu_ref[0, :, c0:c1])).astype(BF16)
        part = _dot(a, wd_ref[0, c0:c1, :])
        h = part if h is None else h + part
    return _layer_norm(alpha * x1 + (1.0 + _mod_slice(m, 5)) * h, g2_ref[0], b2_ref[0])


def _token_order(ref, scr_ref, slot, p):
    f2, n = ref.shape[2], ref.shape[3]
    if f2 == 1:
        return ref[0, p, 0]
    for b in range(f2):
        scr_ref[slot, pl.ds(b, n, stride=f2), :] = ref[0, p, b]
    return scr_ref[slot]


def _post_a_kernel(x_ref, mod_ref, ycv_ref, o0_ref, o1_ref, o2_ref, l0_ref, l1_ref, l2_ref,
                   gc_ref, ga_ref, wpc_ref, bpc_ref, wpa_ref, bpa_ref, wout_ref, bout_ref,
                   g1_ref, b1_ref, out_ref, scr_ref, *, alpha):
    o_refs = (o0_ref, o1_ref, o2_ref)
    l_refs = (l0_ref, l1_ref, l2_ref)
    chunks = []
    for p in range(ATT_WIDTH // LANES):
        os_ = [_token_order(o_refs[g], scr_ref, 2 * g, p) for g in range(N_GROUPS)]
        lses = [_token_order(l_refs[g], scr_ref, 2 * g + 1, p) for g in range(N_GROUPS)]
        chunks.append(_combine_groups(os_, lses).astype(BF16))
    o_att = jnp.concatenate(chunks, axis=1)
    out_ref[0] = _merge_to_ln1(x_ref[0], mod_ref[0], ycv_ref[0], o_att,
                               gc_ref[0].astype(F32), ga_ref[0].astype(F32),
                               wpc_ref, bpc_ref, wpa_ref, bpa_ref, wout_ref, bout_ref,
                               g1_ref, b1_ref, alpha)


def _post_a(x, mod, ycv, os_, lses, gate_c, gate_a, wts, layer, tm, alpha):
    bsz, s, d = x.shape
    nt = s // tm
    tile = lambda width: pl.BlockSpec((1, tm, width), lambda b, i: (b, i, 0))
    nslab = ATT_WIDTH // LANES
    splits = [_att_split(dil)[1] for dil in DILATIONS]
    slab_specs = [pl.BlockSpec((1, nslab, f2, tm // f2, LANES), lambda b, i: (b, 0, 0, i, 0)) for f2 in splits]
    view = lambda arrs: [a.reshape(bsz, nslab, f2, s // f2, LANES) for a, f2 in zip(arrs, splits)]
    return pl.pallas_call(
        functools.partial(_post_a_kernel, alpha=alpha),
        grid=(bsz, nt),
        in_specs=([tile(d), pl.BlockSpec((1, 1, MOD_WIDTH), lambda b, i: (b, 0, 0)), tile(CONV_CH)]
                  + slab_specs + slab_specs + [tile(d), tile(d)]
                  + [_layer_spec(w, layer) for w in wts]),
        out_specs=tile(d),
        out_shape=jax.ShapeDtypeStruct((bsz, s, d), F32),
        scratch_shapes=[pltpu.VMEM((2 * N_GROUPS, tm, LANES), F32)],
        compiler_params=_params(2),
        name="post_a",
    )(x, mod, ycv, *view(os_), *view(lses), gate_c, gate_a, *wts)


def _post_b_kernel(*refs, alpha, chunk, t_new, nsq, nblk):
    x_ref, mod_ref, wg_ref, wu_ref, wd_ref, g2_ref, b2_ref = refs[:7]
    q_refs, c_refs, n_refs = refs[7:10], refs[10:13], refs[13:16]
    bc_refs, bn_refs = refs[16:19], refs[19:22]
    out_ref, o_ref = refs[-5:-3]
    oc_refs = refs[-3:]
    out_ref[0] = _ffn_to_ln2(x_ref[0], mod_ref[0], wg_ref, wu_ref, wd_ref, g2_ref, b2_ref, alpha, chunk)
    step = pl.program_id(0) * pl.num_programs(1) + pl.program_id(1)
    first_seq = (step % nblk) * nsq
    for i in range(nsq):
        _sample_unit(q_refs, c_refs, n_refs, bc_refs, bn_refs, o_ref, oc_refs, i, first_seq + i, t_new)


def _post_b(x1, mod, wts, layer, tm, alpha, q8, caches_n, kvt_new, prev_outs, nseq, t_new):
    bsz, s, d = x1.shape
    nt = s // tm
    hsteps = ATT_WIDTH // LANES
    steps = bsz * nt
    assert (hsteps * nseq) % steps == 0 and nseq * t_new == LANES
    nsq = hsteps * nseq // steps
    nblk = nseq // nsq
    side = lambda f: (lambda b, i: f(*divmod(b * nt + i, nblk)))
    tables = [_sample_bias(g, caches_n[g].shape[-1], t_new) for g in range(N_GROUPS)]
    bc = [jnp.asarray(tc.reshape(hsteps, 2, Q_ROWS, -1)) for tc, _ in tables]
    bn = [jnp.asarray(tn.reshape(hsteps, 2, Q_ROWS, LANES)) for _, tn in tables]
    kvt4 = kvt_new.reshape(N_GROUPS, 2, ATT_WIDTH, nseq * t_new)
    tile = pl.BlockSpec((1, tm, d), lambda b, i: (b, i, 0))
    args = [x1, mod, *wts]
    in_specs = ([tile, pl.BlockSpec((1, 1, MOD_WIDTH), lambda b, i: (b, 0, 0))]
                + [_layer_spec(w, layer) for w in wts])
    for g in range(N_GROUPS):
        in_specs.append(pl.BlockSpec((nsq, Q_ROWS, LANES), side(lambda hc, blk, g=g: (blk, 0, g * hsteps + hc))))
        args.append(q8)
    cache_specs = [pl.BlockSpec((nsq, 2, LANES, c.shape[-1]),
                                side(lambda hc, blk: (layer * nblk + blk, 0, hc, 0))) for c in caches_n]
    in_specs += cache_specs
    args += list(caches_n)
    for g in range(N_GROUPS):
        in_specs.append(pl.BlockSpec((1, 2, LANES, LANES), side(lambda hc, blk, g=g: (g, 0, hc, 0))))
        args.append(kvt4)
    for tbl in bc + bn:
        in_specs.append(pl.BlockSpec((1,) + tbl.shape[1:], side(lambda hc, blk: (hc, 0, 0, 0))))
        args.append(tbl)
    aliases = {}
    if prev_outs is not None:
        for g in range(N_GROUPS):
            in_specs.append(pl.BlockSpec(memory_space=pl.ANY))
            aliases[len(args)] = 2 + g
            args.append(prev_outs[g])
    outs = pl.pallas_call(
        functools.partial(_post_b_kernel, alpha=alpha, chunk=1024, t_new=t_new, nsq=nsq, nblk=nblk),
        grid=(bsz, nt),
        in_specs=in_specs,
        out_specs=[tile, pl.BlockSpec((nsq, Q_ROWS, LANES), side(lambda hc, blk: (blk, 0, hc)))] + cache_specs,
        out_shape=[jax.ShapeDtypeStruct((bsz, s, d), F32),
                   jax.ShapeDtypeStruct((nseq, Q_ROWS, ATT_WIDTH), F32)]
                  + [jax.ShapeDtypeStruct(c.shape, F32) for c in caches_n],
        input_output_aliases=aliases,
        compiler_params=_params(2),
        name="post_b",
    )(*args)
    return outs[0], outs[1], outs[2:]


def _post_s_kernel(x_ref, mod_ref, state_ref, glu_ref, o_ref, gc_ref, ga_ref,
                   wdw_ref, bdw_ref, cg_ref, cb_ref,
                   wpc_ref, bpc_ref, wpa_ref, bpa_ref, wout_ref, bout_ref, g1_ref, b1_ref,
                   wg_ref, wu_ref, wd_ref, g2_ref, b2_ref,
                   out_ref, nconv_ref, ypre_ref, *, nseq, t_new, alpha, chunk):
    ctx = CONV_WIDTH - 1

    def slab(i):
        if i < ctx:
            return state_ref[0, i]
        return glu_ref[0, (i - ctx) * nseq:(i - ctx + 1) * nseq, :]

    for t in range(t_new):
        acc = jnp.zeros((nseq, CONV_CH), F32) + bdw_ref[0]
        for j in range(CONV_WIDTH):
            acc = acc + slab(t + j) * wdw_ref[0, j:j + 1, :]
        ypre_ref[t * nseq:(t + 1) * nseq, :] = acc
    for i in range(ctx):
        nconv_ref[0, i] = slab(i + t_new)
    ybuf = _silu(_layer_norm(ypre_ref[...], cg_ref[0], cb_ref[0])).astype(BF16)
    m = mod_ref[0]
    x1 = _merge_to_ln1(x_ref[0], m, ybuf, o_ref[0], gc_ref[0].astype(F32), ga_ref[0].astype(F32),
                       wpc_ref, bpc_ref, wpa_ref, bpa_ref, wout_ref, bout_ref, g1_ref, b1_ref, alpha)
    out_ref[0] = _ffn_to_ln2(x1, m, wg_ref, wu_ref, wd_ref, g2_ref, b2_ref, alpha, chunk)


def _post_s(x, mod, state_n, layer, glu, o_att, gate_c, gate_a, wts, nseq, t_new, alpha):
    _, m, d = x.shape
    ctx = CONV_WIDTH - 1
    full = lambda *shape: _const_spec(shape)
    whole = lambda *shape: pl.BlockSpec(shape, lambda i: (0,) * len(shape))
    return pl.pallas_call(
        functools.partial(_post_s_kernel, nseq=nseq, t_new=t_new, alpha=alpha, chunk=1024),
        in_specs=[full(1, m, d), full(1, m, MOD_WIDTH), _layer_spec(state_n, layer), full(1, m, CONV_CH),
                  full(1, m, ATT_WIDTH), full(1, m, d), full(1, m, d)]
                 + [_layer_spec(w, layer) for w in wts],
        out_specs=[whole(1, m, d), whole(1, ctx, nseq, CONV_CH)],
        out_shape=[jax.ShapeDtypeStruct((1, m, d), F32),
                   jax.ShapeDtypeStruct((1, ctx, nseq, CONV_CH), F32)],
        scratch_shapes=[pltpu.VMEM((m, CONV_CH), F32)],
        grid=(1,),
        compiler_params=_params(1),
        name="post_s",
    )(x, mod, state_n, glu, o_att, gate_c, gate_a, *wts)


def _sample_bias(g, length, t_new):
    dil = DILATIONS[g]
    slopes = _alibi_slopes()[:, None, None]
    t = np.arange(Q_ROWS)[:, None]
    p = np.arange(length)[None, :]
    dist = length + t - p
    valid = (dist % dil == 0) & (dist // dil >= 1) & (dist // dil <= SPAN) & (t < t_new)
    cache = np.where(valid[None], -slopes * dist.astype(np.float32)[None], np.float32(NEG))
    tp = np.arange(LANES)[None, :] - (LANES - t_new)
    dist_n = t - tp
    valid_n = (tp >= 0) & (dist_n >= 0) & (dist_n % dil == 0) & (dist_n // dil <= SPAN) & (t < t_new)
    new = np.where(valid_n[None], -slopes * dist_n.astype(np.float32)[None], np.float32(NEG))
    return cache.astype(np.float32), new.astype(np.float32)


def _sample_unit(q_refs, c_refs, n_refs, bc_refs, bn_refs, o_ref, oc_refs, i, seq, t_new):
    shift = (LANES - t_new) - t_new * seq
    lane = lax.broadcasted_iota(jnp.int32, (LANES, LANES), 1)
    heads_per_step = LANES // HEAD_DIM
    per_head = [[] for _ in range(heads_per_step)]
    for g in range(N_GROUPS):
        c_ref = c_refs[g]
        length = c_ref.shape[-1]
        new = [pltpu.roll(n_refs[g][0, kv], shift, axis=1) for kv in range(2)]
        for e in range(heads_per_step):
            rows = slice(e * HEAD_DIM, (e + 1) * HEAD_DIM)
            qh = q_refs[g][i, :, rows].astype(BF16)
            s_c = _dot(qh, c_ref[i, 0, rows, :].astype(BF16)) + bc_refs[g][0, e]
            s_n = _dot(qh, new[0][rows, :].astype(BF16)) + bn_refs[g][0, e]
            m = jnp.maximum(jnp.max(s_c, axis=-1, keepdims=True), jnp.max(s_n, axis=-1, keepdims=True))
            p_c = jnp.exp(s_c - m)
            p_n = jnp.exp(s_n - m)
            l = jnp.sum(p_c, axis=-1, keepdims=True) + jnp.sum(p_n, axis=-1, keepdims=True)
            o = (_dot_nt(p_c.astype(BF16), c_ref[i, 1, rows, :].astype(BF16))
                 + _dot_nt(p_n.astype(BF16), new[1][rows, :].astype(BF16)))
            per_head[e].append((o / l, m + jnp.log(l)))
        for kv in range(2):
            rolled = pltpu.roll(c_ref[i, kv], length - t_new, axis=1)
            if length > LANES:
                oc_refs[g][i, kv, :, 0:length - LANES] = rolled[:, 0:length - LANES]
            oc_refs[g][i, kv, :, length - LANES:length] = jnp.where(
                lane >= LANES - t_new, new[kv], rolled[:, length - LANES:length])
    for e in range(heads_per_step):
        os_, lses = zip(*per_head[e])
        o_ref[i, :, e * HEAD_DIM:(e + 1) * HEAD_DIM] = _combine_groups(list(os_), list(lses))


def _native_view(cache):
    depth, nseq, length = cache.shape[:3]
    return cache.transpose(0, 1, 3, 4, 5, 2).reshape(depth * nseq, 2, ATT_WIDTH, length)


def _from_native(x, depth, nseq):
    length = x.shape[-1]
    return x.reshape(depth, nseq, 2, N_SLOTS, HEAD_DIM, length).transpose(0, 1, 5, 2, 3, 4)


def kernel(x_prompt, x_sample, c_prompt, c_sample, cache_kv_g0, cache_kv_g1, cache_kv_g2, state_conv, w_ada, b_ada, w_in, b_in, w_dw, b_dw, conv_ln_g, conv_ln_b, w_pc, b_pc, w_pa, b_pa, w_out, b_out, ln1_g, ln1_b, w_gate, w_up, w_down, ln2_g, ln2_b):
    depth = w_in.shape[0]
    bsz, seq, d = x_prompt.shape
    nseq, t_new, _ = x_sample.shape
    alpha = (2 * depth) ** 0.25
    tm = 512
    tm_a = 1024
    tm_ffn = 256
    m_s = nseq * t_new

    mod_all = _adaln(jnp.concatenate([c_prompt, c_sample], axis=0), w_ada, b_ada)
    caches_n = [_native_view(c) for c in (cache_kv_g0, cache_kv_g1, cache_kv_g2)]
    state_n = state_conv.transpose(0, 2, 1, 3)

    vec = lambda v: v.reshape(depth, 1, -1)
    w_in_bf = w_in.astype(BF16)
    b_in3 = vec(b_in)
    conv_w = [w_dw, vec(b_dw), vec(conv_ln_g), vec(conv_ln_b)]
    merge_w = [w_pc.astype(BF16), vec(b_pc), w_pa.astype(BF16), vec(b_pa),
               w_out.astype(BF16), vec(b_out), vec(ln1_g), vec(ln1_b)]
    ffn_w = [w_gate.astype(BF16), w_up.astype(BF16), w_down.astype(BF16), vec(ln2_g), vec(ln2_b)]

    xp = x_prompt
    xs = x_sample.transpose(1, 0, 2).reshape(1, m_s, d)
    conv_prompt, conv_sample = [], []
    cache_outs = None
    kv_nat = None
    for l in range(depth):
        mod_p = mod_all[l, :bsz].reshape(bsz, 1, MOD_WIDTH)
        mod_s = jnp.tile(mod_all[l, bsz:], (t_new, 1)).reshape(1, m_s, MOD_WIDTH)
        mod_sq = jnp.repeat(mod_all[l, bsz:], t_new, axis=0).reshape(1, m_s, MOD_WIDTH)

        xs_q = xs.reshape(t_new, nseq, d).transpose(1, 0, 2).reshape(1, m_s, d)
        glu_s, gc_s, ga_s, q_s, kvt_new = _inproj_s(xs, mod_s, xs_q, mod_sq, w_in_bf, b_in3, l)
        q8 = jnp.pad(q_s.reshape(t_new, nseq, N_GROUPS * ATT_WIDTH).transpose(1, 0, 2),
                     ((0, 0), (0, Q_ROWS - t_new), (0, 0)))

        glu, gate_c, gate_a, *qx = _inproj_a(xp, mod_p, w_in_bf, b_in3, l, tm_a)
        kvb, kv_nat, ycv = _inproj_b(xp, mod_p, w_in_bf, b_in3, glu, conv_w, l, depth, kv_nat, tm)
        conv_prompt.append(glu[:, seq - (CONV_WIDTH - 1):, :])
        os_, lses = [], []
        for g in range(N_GROUPS):
            o, lse = _prompt_attention(qx[g], kvb[g], DILATIONS[g])
            os_.append(o)
            lses.append(lse)
        x1 = _post_a(xp, mod_p, ycv, os_, lses, gate_c, gate_a, merge_w, l, tm, alpha)
        xp, o8, cache_outs = _post_b(x1, mod_p, ffn_w, l, tm_ffn, alpha,
                                     q8, caches_n, kvt_new, cache_outs, nseq, t_new)

        o_s = o8[:, :t_new].transpose(1, 0, 2).reshape(1, m_s, ATT_WIDTH)
        xs, nconv = _post_s(xs, mod_s, state_n, l, glu_s, o_s, gc_s, ga_s, conv_w + merge_w + ffn_w,
                            nseq, t_new, alpha)
        conv_sample.append(nconv[0])

    kv_p = [_from_native(kv_nat[g].reshape(depth * bsz, 2, ATT_WIDTH, -1), depth, bsz)
            for g in range(N_GROUPS)]
    kv_s = [_from_native(cache_outs[g], depth, nseq) for g in range(N_GROUPS)]
    return (xp, xs.reshape(t_new, nseq, d).transpose(1, 0, 2),
            kv_p[0], kv_p[1], kv_p[2], jnp.stack(conv_prompt),
            kv_s[0], kv_s[1], kv_s[2], jnp.stack(conv_sample).transpose(0, 2, 1, 3))
```

```python
import functools

import numpy as np
import jax
import jax.numpy as jnp
from jax import lax
from jax.experimental import pallas as pl
from jax.experimental.pallas import tpu as pltpu

D_MODEL = 1024
CONV_CH = 512
CONV_WIDTH = 31
N_SLOTS = 8
HEAD_DIM = 64
ATT_WIDTH = N_SLOTS * HEAD_DIM
WINDOWS = (128, 512, 2048)
DILATIONS = (1, 4, 16)
N_GROUPS = 3
SPAN = 128
BLOCK = 128
LN_EPS = 1e-5
QKV_OFF = 2 * CONV_CH
GATE_OFF = QKV_OFF + 3 * N_GROUPS * ATT_WIDTH
IN_WIDTH = GATE_OFF + 2 * D_MODEL
KV_WIDTH = 2 * ATT_WIDTH
MOD_WIDTH = 6 * D_MODEL
LANES = 128
NEG = -1e30
Q_ROWS = 8
CONV_HALO = 32
VMEM_LIMIT = 56 * 1024 * 1024

F32 = jnp.float32
BF16 = jnp.bfloat16
GATE_DTYPE = jnp.bfloat16

_NT = (((1,), (1,)), ((), ()))


def _q_col(g):
    return QKV_OFF + g * ATT_WIDTH


def _k_col(g):
    return QKV_OFF + (N_GROUPS + g) * ATT_WIDTH


def _v_col(g):
    return QKV_OFF + (2 * N_GROUPS + g) * ATT_WIDTH


def _dot(a, b):
    return jnp.dot(a, b, preferred_element_type=F32)


def _dot_nt(a, b):
    return lax.dot_general(a, b, _NT, preferred_element_type=F32)


def _sigmoid(x):
    return 0.5 * jnp.tanh(0.5 * x) + 0.5


def _silu(x):
    return x * _sigmoid(x)


def _layer_norm(x, g, b):
    mu = jnp.mean(x, axis=-1, keepdims=True)
    xc = x - mu
    var = jnp.mean(xc * xc, axis=-1, keepdims=True)
    return xc * lax.rsqrt(var + LN_EPS) * g + b


def _mod_slice(m, i):
    return m[:, i * D_MODEL:(i + 1) * D_MODEL]


def _const_spec(shape):
    nd = len(shape)
    return pl.BlockSpec(shape, lambda *_: (0,) * nd, pipeline_mode=pl.Buffered(1))


def _layer_spec(arr, layer):
    nd = arr.ndim
    return pl.BlockSpec((1,) + arr.shape[1:], lambda *_: (layer,) + (0,) * (nd - 1),
                        pipeline_mode=pl.Buffered(1))


def _params(n_axes):
    return pltpu.CompilerParams(dimension_semantics=("arbitrary",) * n_axes,
                                vmem_limit_bytes=VMEM_LIMIT)


def _adaln_kernel(c_ref, w_ref, b_ref, o_ref):
    c = c_ref[...]
    s = _silu(c).astype(BF16)
    o_ref[0] = _dot(s, w_ref[0].astype(BF16)) + b_ref[0]


def _adaln(c_all, w_ada, b_ada):
    depth, d, width = w_ada.shape
    rows = c_all.shape[0]
    tn = 1024
    return pl.pallas_call(
        _adaln_kernel,
        grid=(depth, width // tn),
        in_specs=[pl.BlockSpec((rows, d), lambda l, j: (0, 0)),
                  pl.BlockSpec((1, d, tn), lambda l, j: (l, 0, j)),
                  pl.BlockSpec((1, 1, tn), lambda l, j: (l, 0, j))],
        out_specs=pl.BlockSpec((1, rows, tn), lambda l, j: (l, 0, j)),
        out_shape=jax.ShapeDtypeStruct((depth, rows, width), F32),
        compiler_params=_params(2),
        name="adaln",
    )(c_all, w_ada, b_ada.reshape(depth, 1, width))


def _modulated(x_ref, mod_ref, shift_i, scale_i):
    m = mod_ref[0]
    x = x_ref[0]
    return (x * (1.0 + _mod_slice(m, scale_i)) + _mod_slice(m, shift_i)).astype(BF16)


def _proj(u, w_ref, b_ref, c0, width):
    return _dot(u, w_ref[0, :, c0:c0 + width]) + b_ref[0, :, c0:c0 + width]


def _glu_gates_q(u, w_ref, b_ref):
    glu = _proj(u, w_ref, b_ref, 0, CONV_CH) * _sigmoid(_proj(u, w_ref, b_ref, CONV_CH, CONV_CH))
    gate_c = _sigmoid(_proj(u, w_ref, b_ref, GATE_OFF, D_MODEL))
    gate_a = _sigmoid(_proj(u, w_ref, b_ref, GATE_OFF + D_MODEL, D_MODEL))
    qs = [_proj(u, w_ref, b_ref, _q_col(g), ATT_WIDTH) * (HEAD_DIM ** -0.5) for g in range(N_GROUPS)]
    return glu, gate_c, gate_a, qs


def _residue_rows(t, scr_ref, slot, dil):
    if dil == 1:
        return [t]
    rows = t.shape[0]
    scr_ref[slot, 0:rows, :] = t
    if dil <= 4:
        return [scr_ref[slot, pl.ds(r, rows // dil, stride=dil), :] for r in range(dil)]
    assert dil == 16
    quarter = rows // 4
    for b in range(4):
        scr_ref[slot, rows + b * quarter:rows + (b + 1) * quarter, :] = scr_ref[slot, pl.ds(b, quarter, stride=4), :]
    return [scr_ref[slot, pl.ds(rows + (r % 4) * quarter + r // 4, rows // dil, stride=4), :] for r in range(dil)]


def _inproj_a_kernel(x_ref, mod_ref, w_ref, b_ref, glu_ref, gc_ref, ga_ref, q0_ref, q1_ref, q2_ref,
                     scr_ref):
    u = _modulated(x_ref, mod_ref, 0, 1)
    glu, gate_c, gate_a, qs = _glu_gates_q(u, w_ref, b_ref)
    glu_ref[0] = glu
    gc_ref[0] = gate_c.astype(gc_ref.dtype)
    ga_ref[0] = gate_a.astype(ga_ref.dtype)
    for g, (q, q_ref) in enumerate(zip(qs, (q0_ref, q1_ref, q2_ref))):
        dil = DILATIONS[g]
        for p in range(ATT_WIDTH // LANES):
            parts = _residue_rows(q[:, p * LANES:(p + 1) * LANES], scr_ref, p, dil)
            for r, t in enumerate(parts):
                lane = lax.broadcasted_iota(jnp.int32, t.shape, 1)
                q_ref[0, r, :, (2 * p) * LANES:(2 * p + 1) * LANES] = jnp.where(lane < HEAD_DIM, t, 0.0).astype(BF16)
                q_ref[0, r, :, (2 * p + 1) * LANES:(2 * p + 2) * LANES] = jnp.where(lane >= HEAD_DIM, t, 0.0).astype(BF16)


def _residue_spec(tm, dil, width):
    return pl.BlockSpec((1, dil, tm // dil, width), lambda b, i: (b, 0, i, 0))


def _inproj_a(x, mod, w_in, b_in, layer, tm):
    bsz, s, d = x.shape
    nt = s // tm
    tile = lambda width: pl.BlockSpec((1, tm, width), lambda b, i: (b, i, 0))
    out_shape = ([jax.ShapeDtypeStruct((bsz, s, CONV_CH), F32),
                  jax.ShapeDtypeStruct((bsz, s, d), GATE_DTYPE),
                  jax.ShapeDtypeStruct((bsz, s, d), GATE_DTYPE)]
                 + [jax.ShapeDtypeStruct((bsz, dil, s // dil, 2 * ATT_WIDTH), BF16) for dil in DILATIONS])
    return pl.pallas_call(
        _inproj_a_kernel,
        grid=(bsz, nt),
        in_specs=[tile(d),
                  pl.BlockSpec((1, 1, MOD_WIDTH), lambda b, i: (b, 0, 0)),
                  _layer_spec(w_in, layer), _layer_spec(b_in, layer)],
        out_specs=[tile(CONV_CH), tile(d), tile(d)]
                  + [_residue_spec(tm, dil, 2 * ATT_WIDTH) for dil in DILATIONS],
        out_shape=out_shape,
        scratch_shapes=[pltpu.VMEM((ATT_WIDTH // LANES, 2 * tm, LANES), F32)],
        compiler_params=_params(2),
        name="inproj_a",
    )(x, mod, w_in, b_in)


def _conv_ln_silu(win_ref, r0, rb, wdw_ref, bdw_ref, cg_ref, cb_ref, out_ref):
    off = CONV_HALO - (CONV_WIDTH - 1)
    accs = []
    for c in range(CONV_CH // LANES):
        cols = slice(c * LANES, (c + 1) * LANES)
        acc = jnp.zeros((rb, LANES), F32) + bdw_ref[0, :, cols]
        for j in range(CONV_WIDTH):
            acc = acc + win_ref[c, pl.ds(r0 + off + j, rb, stride=1), :] * wdw_ref[0, j:j + 1, cols]
        accs.append(acc)
    y = _silu(_layer_norm(jnp.concatenate(accs, axis=1), cg_ref[0], cb_ref[0]))
    out_ref[0, r0:r0 + rb, :] = y.astype(out_ref.dtype)


def _inproj_b_kernel(*refs, tm, seq, aliased):
    (x_ref, mod_ref, w_ref, b_ref, glu_ref, halo_ref,
     wdw_ref, bdw_ref, cg_ref, cb_ref) = refs[:10]
    n_in = 10 + (N_GROUPS if aliased else 0)
    kv_refs = refs[n_in:n_in + N_GROUPS]
    nat_refs = refs[n_in + N_GROUPS:n_in + 2 * N_GROUPS]
    ycv_ref, scr_ref, win_ref, keep_ref = refs[n_in + 2 * N_GROUPS:]
    ti = pl.program_id(1)
    halo = halo_ref[0]
    halo = jnp.where(ti > 0, halo, jnp.zeros_like(halo))
    glu = glu_ref[0]
    for c in range(CONV_CH // LANES):
        win_ref[c, 0:CONV_HALO, :] = halo[:, c * LANES:(c + 1) * LANES]
        win_ref[c, CONV_HALO:CONV_HALO + tm, :] = glu[:, c * LANES:(c + 1) * LANES]
    u = _modulated(x_ref, mod_ref, 0, 1)
    wins = [min(w, seq) for w in WINDOWS]
    every_tile = [w >= seq for w in wins]

    rb = 64
    for r0 in range(0, tm, rb):
        _conv_ln_silu(win_ref, r0, rb, wdw_ref, bdw_ref, cg_ref, cb_ref, ycv_ref)
    for g, kv_ref in enumerate(kv_refs):
        for half, c0 in enumerate((_k_col(g), _v_col(g))):
            t = _proj(u, w_ref, b_ref, c0, ATT_WIDTH)
            for p in range(ATT_WIDTH // LANES):
                parts = _residue_rows(t[:, p * LANES:(p + 1) * LANES], scr_ref, half * 4 + p, DILATIONS[g])
                for r, part in enumerate(parts):
                    cols = slice(half * ATT_WIDTH + p * LANES, half * ATT_WIDTH + (p + 1) * LANES)
                    kv_ref[0, r, :, cols] = part.astype(BF16)
            if every_tile[g]:
                nat_refs[g][0, half * ATT_WIDTH:(half + 1) * ATT_WIDTH, :] = t.T
            else:
                keep_ref[2 * g + half] = t
    for g, n_ref in enumerate(nat_refs):
        if every_tile[g]:
            continue
        win = wins[g]
        rows = min(win, tm)
        cond = (ti >= (seq - win) // tm) if win >= tm else (ti == seq // tm - 1)

        @pl.when(cond)
        def _(g=g, n_ref=n_ref, rows=rows):
            for half in range(2):
                n_ref[0, half * ATT_WIDTH:(half + 1) * ATT_WIDTH, :] = keep_ref[2 * g + half, tm - rows:tm, :].T


def _inproj_b(x, mod, w_in, b_in, glu, conv_w, layer, depth, prev_nat, tm):
    bsz, s, d = x.shape
    nt = s // tm
    assert tm % CONV_HALO == 0
    tile = lambda width: pl.BlockSpec((1, tm, width), lambda b, i: (b, i, 0))
    halo_spec = pl.BlockSpec(
        (1, CONV_HALO, CONV_CH), lambda b, i: (b, jnp.maximum(i * (tm // CONV_HALO) - 1, 0), 0))
    nat_specs, nat_shapes = [], []
    for g in range(N_GROUPS):
        win = min(WINDOWS[g], s)
        assert win % tm == 0 or tm % win == 0
        nat_shapes.append(jax.ShapeDtypeStruct((depth * bsz, KV_WIDTH, win), F32))
        if win >= tm:
            first = (s - win) // tm
            nat_specs.append(pl.BlockSpec(
                (1, KV_WIDTH, tm),
                lambda b, i, first=first: (layer * bsz + b, 0, jnp.maximum(i - first, 0))))
        else:
            nat_specs.append(pl.BlockSpec((1, KV_WIDTH, win), lambda b, i: (layer * bsz + b, 0, 0)))
    args = [x, mod, w_in, b_in, glu, glu, *conv_w]
    in_specs = [tile(d), pl.BlockSpec((1, 1, MOD_WIDTH), lambda b, i: (b, 0, 0)),
                _layer_spec(w_in, layer), _layer_spec(b_in, layer),
                tile(CONV_CH), halo_spec] + [_layer_spec(w, layer) for w in conv_w]
    aliases = {}
    if prev_nat is not None:
        for g in range(N_GROUPS):
            in_specs.append(pl.BlockSpec(memory_space=pl.ANY))
            aliases[len(args)] = N_GROUPS + g
            args.append(prev_nat[g])
    outs = pl.pallas_call(
        functools.partial(_inproj_b_kernel, tm=tm, seq=s, aliased=prev_nat is not None),
        grid=(bsz, nt),
        in_specs=in_specs,
        out_specs=[_residue_spec(tm, dil, KV_WIDTH) for dil in DILATIONS] + nat_specs + [tile(CONV_CH)],
        out_shape=[jax.ShapeDtypeStruct((bsz, dil, s // dil, KV_WIDTH), BF16) for dil in DILATIONS]
                  + nat_shapes + [jax.ShapeDtypeStruct((bsz, s, CONV_CH), BF16)],
        scratch_shapes=[pltpu.VMEM((KV_WIDTH // LANES, 2 * tm, LANES), F32),
                        pltpu.VMEM((CONV_CH // LANES, CONV_HALO + tm, LANES), F32),
                        pltpu.VMEM((2 * N_GROUPS, tm, ATT_WIDTH), F32)],
        input_output_aliases=aliases,
        compiler_params=_params(2),
        name="inproj_b",
    )(*args)
    return outs[:N_GROUPS], outs[N_GROUPS:2 * N_GROUPS], outs[2 * N_GROUPS]


def _inproj_s_kernel(x_ref, mod_ref, xq_ref, modq_ref, w_ref, b_ref,
                     glu_ref, gc_ref, ga_ref, q_ref, kvt_ref):
    u = _modulated(x_ref, mod_ref, 0, 1)
    glu, gate_c, gate_a, qs = _glu_gates_q(u, w_ref, b_ref)
    glu_ref[0] = glu
    gc_ref[0] = gate_c.astype(gc_ref.dtype)
    ga_ref[0] = gate_a.astype(ga_ref.dtype)
    for g, q in enumerate(qs):
        q_ref[0, :, g * ATT_WIDTH:(g + 1) * ATT_WIDTH] = q
    uq = _modulated(xq_ref, modq_ref, 0, 1)
    for g in range(N_GROUPS):
        for half, c0 in enumerate((_k_col(g), _v_col(g))):
            r0 = g * KV_WIDTH + half * ATT_WIDTH
            kvt_ref[r0:r0 + ATT_WIDTH, :] = _proj(uq, w_ref, b_ref, c0, ATT_WIDTH).T


def _inproj_s(x, mod, xq, modq, w_in, b_in, layer):
    _, m, d = x.shape
    full = lambda *shape: _const_spec(shape)
    whole = lambda *shape: pl.BlockSpec(shape, lambda i: (0,) * len(shape))
    return pl.pallas_call(
        _inproj_s_kernel,
        in_specs=[full(1, m, d), full(1, m, MOD_WIDTH), full(1, m, d), full(1, m, MOD_WIDTH),
                  _layer_spec(w_in, layer), _layer_spec(b_in, layer)],
        out_specs=[whole(1, m, CONV_CH), whole(1, m, d), whole(1, m, d),
                   whole(1, m, N_GROUPS * ATT_WIDTH), whole(N_GROUPS * KV_WIDTH, m)],
        out_shape=[jax.ShapeDtypeStruct((1, m, CONV_CH), F32),
                   jax.ShapeDtypeStruct((1, m, d), GATE_DTYPE),
                   jax.ShapeDtypeStruct((1, m, d), GATE_DTYPE),
                   jax.ShapeDtypeStruct((1, m, N_GROUPS * ATT_WIDTH), F32),
                   jax.ShapeDtypeStruct((N_GROUPS * KV_WIDTH, m), F32)],
        grid=(1,),
        compiler_params=_params(1),
        name="inproj_s",
    )(x, mod, xq, modq, w_in, b_in)


def _alibi_slopes():
    return (2.0 ** (-8.0 * (np.arange(N_SLOTS) + 1) / N_SLOTS)).astype(np.float32)


def _prompt_bias(dil, key_blocks):
    qi = np.arange(BLOCK)[:, None]
    kj = np.arange(key_blocks * BLOCK)[None, :]
    tables = []
    for lead in (0, key_blocks - 1):
        rel = lead * BLOCK + qi - kj
        valid = (rel >= 0) & (rel <= SPAN)
        bias = -_alibi_slopes()[:, None, None] * (rel * dil).astype(np.float32)[None]
        tables.append(np.where(valid[None], bias, np.float32(NEG)))
    return np.stack(tables).astype(np.float32)


def _att_split(dil):
    f1 = min(dil, 4)
    assert dil % f1 == 0
    return f1, dil // f1


def _attn_kernel(q_ref, kv_ref, bias_ref, o_ref, lse_ref, s_scr, m_scr, *, nb, n_units, dil):
    kw = bias_ref.shape[-1]
    seq = o_ref.shape[2]
    lane = lax.broadcasted_iota(jnp.int32, (BLOCK, LANES), 1)
    pick = lambda a, b: jnp.where(lane < HEAD_DIM, a, b)

    def place(u):
        if isinstance(u, int):
            rr, c = divmod(u, nb)
            r0 = c * BLOCK
            k0 = max(r0 - (kw - BLOCK), 0)
            return rr, r0, k0, (0 if c == 0 else 1)
        rr = u // nb
        c = u % nb
        r0 = pl.multiple_of(c * BLOCK, BLOCK)
        k0 = pl.multiple_of(jnp.maximum(r0 - (kw - BLOCK), 0), BLOCK)
        return rr, r0, k0, jnp.minimum(c, 1)

    ones = jnp.ones((kw, LANES), BF16)

    def scores(u, slot):
        rr, r0, k0, tbl = place(u)
        for p in range(ATT_WIDTH // LANES):
            kp = kv_ref[0, rr, pl.ds(k0, kw), p * LANES:(p + 1) * LANES]
            q2 = q_ref[0, rr, pl.ds(r0, BLOCK), 2 * p * LANES:(2 * p + 2) * LANES]
            q2 = jnp.concatenate([q2[:, :LANES], q2[:, LANES:]], axis=0)
            bias2 = jnp.concatenate([bias_ref[tbl, 2 * p], bias_ref[tbl, 2 * p + 1]], axis=0)
            s = _dot_nt(q2, kp) + bias2
            m = jnp.max(s, axis=-1, keepdims=True)
            s_scr[slot, p] = jnp.exp(s - m).astype(BF16)
            m_scr[slot, p] = m

    def finish(u, slot):
        rr, r0, k0, _ = place(u)
        for p in range(ATT_WIDTH // LANES):
            vp = kv_ref[0, rr, pl.ds(k0, kw), ATT_WIDTH + p * LANES:ATT_WIDTH + (p + 1) * LANES]
            m = m_scr[slot, p]
            pv = _dot(s_scr[slot, p], jnp.concatenate([vp, ones], axis=1))
            l2 = pick(pv[:BLOCK, LANES:], pv[BLOCK:, LANES:])
            f1, f2 = _att_split(dil)
            if f1 == 1:
                rows = pl.ds(r0, BLOCK)
            else:
                rows = pl.ds((rr % f2) * (seq // f2) + r0 * f1 + rr // f2, BLOCK, stride=f1)
            o_ref[0, p, rows, :] = pick(pv[:BLOCK, :LANES], pv[BLOCK:, :LANES]) / l2
            lse_ref[0, p, rows, :] = pick(m[:BLOCK], m[BLOCK:]) + jnp.log(l2)

    scores(0, 0)

    def body(u, carry):
        slot = u % 2
        finish(u, slot)
        scores(u + 1, 1 - slot)
        return carry
    lax.fori_loop(0, n_units - 1, body, 0)
    finish(n_units - 1, (n_units - 1) % 2)


def _prompt_attention(qx, kvb, dil):
    bsz, _, n, _ = qx.shape
    assert n % BLOCK == 0
    nb = n // BLOCK
    bias = jnp.asarray(_prompt_bias(dil, min(nb, 2)))
    sub = lambda width: pl.BlockSpec((1, dil, n, width), lambda b: (b, 0, 0, 0))
    slabs = pl.BlockSpec((1, ATT_WIDTH // LANES, dil * n, LANES), lambda b: (b, 0, 0, 0))
    return pl.pallas_call(
        functools.partial(_attn_kernel, nb=nb, n_units=dil * nb, dil=dil),
        grid=(bsz,),
        in_specs=[sub(2 * ATT_WIDTH), sub(KV_WIDTH), _const_spec(bias.shape)],
        out_specs=[slabs, slabs],
        out_shape=[jax.ShapeDtypeStruct((bsz, ATT_WIDTH // LANES, dil * n, LANES), F32)] * 2,
        scratch_shapes=[pltpu.VMEM((2, N_SLOTS // 2, 2 * BLOCK, bias.shape[-1]), BF16),
                        pltpu.VMEM((2, N_SLOTS // 2, 2 * BLOCK, 1), F32)],
        compiler_params=_params(1),
        name=f"attn_d{dil}",
    )(qx, kvb, bias)


def _combine_groups(os_, lses):
    m = functools.reduce(jnp.maximum, lses)
    ws = [jnp.exp(l - m) for l in lses]
    den = functools.reduce(lambda a, b: a + b, ws)
    num = functools.reduce(lambda a, b: a + b, [w * o for w, o in zip(ws, os_)])
    return num / den


def _merge_to_ln1(x, m, ybuf, o_att, gate_c, gate_a,
                  wpc_ref, bpc_ref, wpa_ref, bpa_ref, wout_ref, bout_ref, g1_ref, b1_ref, alpha):
    y_conv = _dot(ybuf, wpc_ref[0]) + bpc_ref[0]
    y_att = _dot(o_att.astype(BF16), wpa_ref[0]) + bpa_ref[0]
    y = (gate_c * y_conv + gate_a * y_att).astype(BF16)
    y = _dot(y, wout_ref[0]) + bout_ref[0]
    return _layer_norm(alpha * x + (1.0 + _mod_slice(m, 2)) * y, g1_ref[0], b1_ref[0])


def _ffn_to_ln2(x1, m, wg_ref, wu_ref, wd_ref, g2_ref, b2_ref, alpha, chunk):
    u2 = (x1 * (1.0 + _mod_slice(m, 4)) + _mod_slice(m, 3)).astype(BF16)
    d_ff = wg_ref.shape[-1]
    h = None
    for c0 in range(0, d_ff, chunk):
        c1 = min(c0 + chunk, d_ff)
        a = (_silu(_dot(u2, wg_ref[0, :, c0:c1])) * _dot(u2, wu_ref[0, :, c0:c1])).astype(BF16)
        part = _dot(a, wd_ref[0, c0:c1, :])
        h = part if h is None else h + part
    return _layer_norm(alpha * x1 + (1.0 + _mod_slice(m, 5)) * h, g2_ref[0], b2_ref[0])


def _token_order(ref, scr_ref, slot, p):
    f2, n = ref.shape[2], ref.shape[3]
    if f2 == 1:
        return ref[0, p, 0]
    for b in range(f2):
        scr_ref[slot, pl.ds(b, n, stride=f2), :] = ref[0, p, b]
    return scr_ref[slot]


def _post_a_kernel(x_ref, mod_ref, ycv_ref, o0_ref, o1_ref, o2_ref, l0_ref, l1_ref, l2_ref,
                   gc_ref, ga_ref, wpc_ref, bpc_ref, wpa_ref, bpa_ref, wout_ref, bout_ref,
                   g1_ref, b1_ref, out_ref, scr_ref, *, alpha):
    o_refs = (o0_ref, o1_ref, o2_ref)
    l_refs = (l0_ref, l1_ref, l2_ref)
    chunks = []
    for p in range(ATT_WIDTH // LANES):
        os_ = [_token_order(o_refs[g], scr_ref, 2 * g, p) for g in range(N_GROUPS)]
        lses = [_token_order(l_refs[g], scr_ref, 2 * g + 1, p) for g in range(N_GROUPS)]
        chunks.append(_combine_groups(os_, lses).astype(BF16))
    o_att = jnp.concatenate(chunks, axis=1)
    out_ref[0] = _merge_to_ln1(x_ref[0], mod_ref[0], ycv_ref[0], o_att,
                               gc_ref[0].astype(F32), ga_ref[0].astype(F32),
                               wpc_ref, bpc_ref, wpa_ref, bpa_ref, wout_ref, bout_ref,
                               g1_ref, b1_ref, alpha)


def _post_a(x, mod, ycv, os_, lses, gate_c, gate_a, wts, layer, tm, alpha):
    bsz, s, d = x.shape
    nt = s // tm
    tile = lambda width: pl.BlockSpec((1, tm, width), lambda b, i: (b, i, 0))
    nslab = ATT_WIDTH // LANES
    splits = [_att_split(dil)[1] for dil in DILATIONS]
    slab_specs = [pl.BlockSpec((1, nslab, f2, tm // f2, LANES), lambda b, i: (b, 0, 0, i, 0)) for f2 in splits]
    view = lambda arrs: [a.reshape(bsz, nslab, f2, s // f2, LANES) for a, f2 in zip(arrs, splits)]
    return pl.pallas_call(
        functools.partial(_post_a_kernel, alpha=alpha),
        grid=(bsz, nt),
        in_specs=([tile(d), pl.BlockSpec((1, 1, MOD_WIDTH), lambda b, i: (b, 0, 0)), tile(CONV_CH)]
                  + slab_specs + slab_specs + [tile(d), tile(d)]
                  + [_layer_spec(w, layer) for w in wts]),
        out_specs=tile(d),
        out_shape=jax.ShapeDtypeStruct((bsz, s, d), F32),
        scratch_shapes=[pltpu.VMEM((2 * N_GROUPS, tm, LANES), F32)],
        compiler_params=_params(2),
        name="post_a",
    )(x, mod, ycv, *view(os_), *view(lses), gate_c, gate_a, *wts)


def _post_b_kernel(*refs, alpha, chunk, t_new, nsq, nblk):
    x_ref, mod_ref, wg_ref, wu_ref, wd_ref, g2_ref, b2_ref = refs[:7]
    q_refs, c_refs, n_refs = refs[7:10], refs[10:13], refs[13:16]
    bc_refs, bn_refs = refs[16:19], refs[19:22]
    out_ref, o_ref = refs[-5:-3]
    oc_refs = refs[-3:]
    out_ref[0] = _ffn_to_ln2(x_ref[0], mod_ref[0], wg_ref, wu_ref, wd_ref, g2_ref, b2_ref, alpha, chunk)
    step = pl.program_id(0) * pl.num_programs(1) + pl.program_id(1)
    first_seq = (step % nblk) * nsq
    for i in range(nsq):
        _sample_unit(q_refs, c_refs, n_refs, bc_refs, bn_refs, o_ref, oc_refs, i, first_seq + i, t_new)


def _post_b(x1, mod, wts, layer, tm, alpha, q8, caches_n, kvt_new, prev_outs, nseq, t_new):
    bsz, s, d = x1.shape
    nt = s // tm
    hsteps = ATT_WIDTH // LANES
    steps = bsz * nt
    assert (hsteps * nseq) % steps == 0 and nseq * t_new == LANES
    nsq = hsteps * nseq // steps
    nblk = nseq // nsq
    side = lambda f: (lambda b, i: f(*divmod(b * nt + i, nblk)))
    tables = [_sample_bias(g, caches_n[g].shape[-1], t_new) for g in range(N_GROUPS)]
    bc = [jnp.asarray(tc.reshape(hsteps, 2, Q_ROWS, -1)) for tc, _ in tables]
    bn = [jnp.asarray(tn.reshape(hsteps, 2, Q_ROWS, LANES)) for _, tn in tables]
    kvt4 = kvt_new.reshape(N_GROUPS, 2, ATT_WIDTH, nseq * t_new)
    tile = pl.BlockSpec((1, tm, d), lambda b, i: (b, i, 0))
    args = [x1, mod, *wts]
    in_specs = ([tile, pl.BlockSpec((1, 1, MOD_WIDTH), lambda b, i: (b, 0, 0))]
                + [_layer_spec(w, layer) for w in wts])
    for g in range(N_GROUPS):
        in_specs.append(pl.BlockSpec((nsq, Q_ROWS, LANES), side(lambda hc, blk, g=g: (blk, 0, g * hsteps + hc))))
        args.append(q8)
    cache_specs = [pl.BlockSpec((nsq, 2, LANES, c.shape[-1]),
                                side(lambda hc, blk: (layer * nblk + blk, 0, hc, 0))) for c in caches_n]
    in_specs += cache_specs
    args += list(caches_n)
    for g in range(N_GROUPS):
        in_specs.append(pl.BlockSpec((1, 2, LANES, LANES), side(lambda hc, blk, g=g: (g, 0, hc, 0))))
        args.append(kvt4)
    for tbl in bc + bn:
        in_specs.append(pl.BlockSpec((1,) + tbl.shape[1:], side(lambda hc, blk: (hc, 0, 0, 0))))
        args.append(tbl)
    aliases = {}
    if prev_outs is not None:
        for g in range(N_GROUPS):
            in_specs.append(pl.BlockSpec(memory_space=pl.ANY))
            aliases[len(args)] = 2 + g
            args.append(prev_outs[g])
    outs = pl.pallas_call(
        functools.partial(_post_b_kernel, alpha=alpha, chunk=1024, t_new=t_new, nsq=nsq, nblk=nblk),
        grid=(bsz, nt),
        in_specs=in_specs,
        out_specs=[tile, pl.BlockSpec((nsq, Q_ROWS, LANES), side(lambda hc, blk: (blk, 0, hc)))] + cache_specs,
        out_shape=[jax.ShapeDtypeStruct((bsz, s, d), F32),
                   jax.ShapeDtypeStruct((nseq, Q_ROWS, ATT_WIDTH), F32)]
                  + [jax.ShapeDtypeStruct(c.shape, F32) for c in caches_n],
        input_output_aliases=aliases,
        compiler_params=_params(2),
        name="post_b",
    )(*args)
    return outs[0], outs[1], outs[2:]


def _post_s_kernel(x_ref, mod_ref, state_ref, glu_ref, o_ref, gc_ref, ga_ref,
                   wdw_ref, bdw_ref, cg_ref, cb_ref,
                   wpc_ref, bpc_ref, wpa_ref, bpa_ref, wout_ref, bout_ref, g1_ref, b1_ref,
                   wg_ref, wu_ref, wd_ref, g2_ref, b2_ref,
                   out_ref, nconv_ref, ypre_ref, *, nseq, t_new, alpha, chunk):
    ctx = CONV_WIDTH - 1

    def slab(i):
        if i < ctx:
            return state_ref[0, i]
        return glu_ref[0, (i - ctx) * nseq:(i - ctx + 1) * nseq, :]

    for t in range(t_new):
        acc = jnp.zeros((nseq, CONV_CH), F32) + bdw_ref[0]
        for j in range(CONV_WIDTH):
            acc = acc + slab(t + j) * wdw_ref[0, j:j + 1, :]
        ypre_ref[t * nseq:(t + 1) * nseq, :] = acc
    for i in range(ctx):
        nconv_ref[0, i] = slab(i + t_new)
    ybuf = _silu(_layer_norm(ypre_ref[...], cg_ref[0], cb_ref[0])).astype(BF16)
    m = mod_ref[0]
    x1 = _merge_to_ln1(x_ref[0], m, ybuf, o_ref[0], gc_ref[0].astype(F32), ga_ref[0].astype(F32),
                       wpc_ref, bpc_ref, wpa_ref, bpa_ref, wout_ref, bout_ref, g1_ref, b1_ref, alpha)
    out_ref[0] = _ffn_to_ln2(x1, m, wg_ref, wu_ref, wd_ref, g2_ref, b2_ref, alpha, chunk)


def _post_s(x, mod, state_n, layer, glu, o_att, gate_c, gate_a, wts, nseq, t_new, alpha):
    _, m, d = x.shape
    ctx = CONV_WIDTH - 1
    full = lambda *shape: _const_spec(shape)
    whole = lambda *shape: pl.BlockSpec(shape, lambda i: (0,) * len(shape))
    return pl.pallas_call(
        functools.partial(_post_s_kernel, nseq=nseq, t_new=t_new, alpha=alpha, chunk=1024),
        in_specs=[full(1, m, d), full(1, m, MOD_WIDTH), _layer_spec(state_n, layer), full(1, m, CONV_CH),
                  full(1, m, ATT_WIDTH), full(1, m, d), full(1, m, d)]
                 + [_layer_spec(w, layer) for w in wts],
        out_specs=[whole(1, m, d), whole(1, ctx, nseq, CONV_CH)],
        out_shape=[jax.ShapeDtypeStruct((1, m, d), F32),
                   jax.ShapeDtypeStruct((1, ctx, nseq, CONV_CH), F32)],
        scratch_shapes=[pltpu.VMEM((m, CONV_CH), F32)],
        grid=(1,),
        compiler_params=_params(1),
        name="post_s",
    )(x, mod, state_n, glu, o_att, gate_c, gate_a, *wts)


def _sample_bias(g, length, t_new):
    dil = DILATIONS[g]
    slopes = _alibi_slopes()[:, None, None]
    t = np.arange(Q_ROWS)[:, None]
    p = np.arange(length)[None, :]
    dist = length + t - p
    valid = (dist % dil == 0) & (dist // dil >= 1) & (dist // dil <= SPAN) & (t < t_new)
    cache = np.where(valid[None], -slopes * dist.astype(np.float32)[None], np.float32(NEG))
    tp = np.arange(LANES)[None, :] - (LANES - t_new)
    dist_n = t - tp
    valid_n = (tp >= 0) & (dist_n >= 0) & (dist_n % dil == 0) & (dist_n // dil <= SPAN) & (t < t_new)
    new = np.where(valid_n[None], -slopes * dist_n.astype(np.float32)[None], np.float32(NEG))
    return cache.astype(np.float32), new.astype(np.float32)


def _sample_unit(q_refs, c_refs, n_refs, bc_refs, bn_refs, o_ref, oc_refs, i, seq, t_new):
    shift = (LANES - t_new) - t_new * seq
    lane = lax.broadcasted_iota(jnp.int32, (LANES, LANES), 1)
    heads_per_step = LANES // HEAD_DIM
    per_head = [[] for _ in range(heads_per_step)]
    for g in range(N_GROUPS):
        c_ref = c_refs[g]
        length = c_ref.shape[-1]
        new = [pltpu.roll(n_refs[g][0, kv], shift, axis=1) for kv in range(2)]
        for e in range(heads_per_step):
            rows = slice(e * HEAD_DIM, (e + 1) * HEAD_DIM)
            qh = q_refs[g][i, :, rows].astype(BF16)
            s_c = _dot(qh, c_ref[i, 0, rows, :].astype(BF16)) + bc_refs[g][0, e]
            s_n = _dot(qh, new[0][rows, :].astype(BF16)) + bn_refs[g][0, e]
            m = jnp.maximum(jnp.max(s_c, axis=-1, keepdims=True), jnp.max(s_n, axis=-1, keepdims=True))
            p_c = jnp.exp(s_c - m)
            p_n = jnp.exp(s_n - m)
            l = jnp.sum(p_c, axis=-1, keepdims=True) + jnp.sum(p_n, axis=-1, keepdims=True)
            o = (_dot_nt(p_c.astype(BF16), c_ref[i, 1, rows, :].astype(BF16))
                 + _dot_nt(p_n.astype(BF16), new[1][rows, :].astype(BF16)))
            per_head[e].append((o / l, m + jnp.log(l)))
        for kv in range(2):
            rolled = pltpu.roll(c_ref[i, kv], length - t_new, axis=1)
            if length > LANES:
                oc_refs[g][i, kv, :, 0:length - LANES] = rolled[:, 0:length - LANES]
            oc_refs[g][i, kv, :, length - LANES:length] = jnp.where(
                lane >= LANES - t_new, new[kv], rolled[:, length - LANES:length])
    for e in range(heads_per_step):
        os_, lses = zip(*per_head[e])
        o_ref[i, :, e * HEAD_DIM:(e + 1) * HEAD_DIM] = _combine_groups(list(os_), list(lses))


def _native_view(cache):
    depth, nseq, length = cache.shape[:3]
    return cache.transpose(0, 1, 3, 4, 5, 2).reshape(depth * nseq, 2, ATT_WIDTH, length)


def _from_native(x, depth, nseq):
    length = x.shape[-1]
    return x.reshape(depth, nseq, 2, N_SLOTS, HEAD_DIM, length).transpose(0, 1, 5, 2, 3, 4)


def kernel(x_prompt, x_sample, c_prompt, c_sample, cache_kv_g0, cache_kv_g1, cache_kv_g2, state_conv, w_ada, b_ada, w_in, b_in, w_dw, b_dw, conv_ln_g, conv_ln_b, w_pc, b_pc, w_pa, b_pa, w_out, b_out, ln1_g, ln1_b, w_gate, w_up, w_down, ln2_g, ln2_b):
    depth = w_in.shape[0]
    bsz, seq, d = x_prompt.shape
    nseq, t_new, _ = x_sample.shape
    alpha = (2 * depth) ** 0.25
    tm = 512
    tm_a = 1024
    tm_ffn = 256
    m_s = nseq * t_new

    mod_all = _adaln(jnp.concatenate([c_prompt, c_sample], axis=0), w_ada, b_ada)
    caches_n = [_native_view(c) for c in (cache_kv_g0, cache_kv_g1, cache_kv_g2)]
    state_n = state_conv.transpose(0, 2, 1, 3)

    vec = lambda v: v.reshape(depth, 1, -1)
    w_in_bf = w_in.astype(BF16)
    b_in3 = vec(b_in)
    conv_w = [w_dw, vec(b_dw), vec(conv_ln_g), vec(conv_ln_b)]
    merge_w = [w_pc.astype(BF16), vec(b_pc), w_pa.astype(BF16), vec(b_pa),
               w_out.astype(BF16), vec(b_out), vec(ln1_g), vec(ln1_b)]
    ffn_w = [w_gate.astype(BF16), w_up.astype(BF16), w_down.astype(BF16), vec(ln2_g), vec(ln2_b)]

    xp = x_prompt
    xs = x_sample.transpose(1, 0, 2).reshape(1, m_s, d)
    conv_prompt, conv_sample = [], []
    cache_outs = None
    kv_nat = None
    for l in range(depth):
        mod_p = mod_all[l, :bsz].reshape(bsz, 1, MOD_WIDTH)
        mod_s = jnp.tile(mod_all[l, bsz:], (t_new, 1)).reshape(1, m_s, MOD_WIDTH)
        mod_sq = jnp.repeat(mod_all[l, bsz:], t_new, axis=0).reshape(1, m_s, MOD_WIDTH)

        xs_q = xs.reshape(t_new, nseq, d).transpose(1, 0, 2).reshape(1, m_s, d)
        glu_s, gc_s, ga_s, q_s, kvt_new = _inproj_s(xs, mod_s, xs_q, mod_sq, w_in_bf, b_in3, l)
        q8 = jnp.pad(q_s.reshape(t_new, nseq, N_GROUPS * ATT_WIDTH).transpose(1, 0, 2),
                     ((0, 0), (0, Q_ROWS - t_new), (0, 0)))

        glu, gate_c, gate_a, *qx = _inproj_a(xp, mod_p, w_in_bf, b_in3, l, tm_a)
        kvb, kv_nat, ycv = _inproj_b(xp, mod_p, w_in_bf, b_in3, glu, conv_w, l, depth, kv_nat, tm)
        conv_prompt.append(glu[:, seq - (CONV_WIDTH - 1):, :])
        os_, lses = [], []
        for g in range(N_GROUPS):
            o, lse = _prompt_attention(qx[g], kvb[g], DILATIONS[g])
            os_.append(o)
            lses.append(lse)
        x1 = _post_a(xp, mod_p, ycv, os_, lses, gate_c, gate_a, merge_w, l, tm, alpha)
        xp, o8, cache_outs = _post_b(x1, mod_p, ffn_w, l, tm_ffn, alpha,
                                     q8, caches_n, kvt_new, cache_outs, nseq, t_new)

        o_s = o8[:, :t_new].transpose(1, 0, 2).reshape(1, m_s, ATT_WIDTH)
        xs, nconv = _post_s(xs, mod_s, state_n, l, glu_s, o_s, gc_s, ga_s, conv_w + merge_w + ffn_w,
                            nseq, t_new, alpha)
        conv_sample.append(nconv[0])

    kv_p = [_from_native(kv_nat[g].reshape(depth * bsz, 2, ATT_WIDTH, -1), depth, bsz)
            for g in range(N_GROUPS)]
    kv_s = [_from_native(cache_outs[g], depth, nseq) for g in range(N_GROUPS)]
    return (xp, xs.reshape(t_new, nseq, d).transpose(1, 0, 2),
            kv_p[0], kv_p[1], kv_p[2], jnp.stack(conv_prompt),
            kv_s[0], kv_s[1], kv_s[2], jnp.stack(conv_sample).transpose(0, 2, 1, 3))
```
